```python
import math
import jax, jax.numpy as jnp
from jax import lax
import numpy as np

D_MODEL = 1024
BATCH = 8
SEQ = 4096
DEPTH = 2

N_BRANCH = 4
BRANCH_WIDTH = D_MODEL // 2
CONV_WIDTH = 4
RMS_EPS = 1e-6
SSD_HEADDIM = 64
SSD_HEADS = BRANCH_WIDTH // SSD_HEADDIM
SSD_GROUPS = 2
SSD_HPG = SSD_HEADS // SSD_GROUPS
SSD_STATE = 64
SSD_CHUNK = 128
SSD_XBC = BRANCH_WIDTH + 2 * SSD_GROUPS * SSD_STATE
GLA_HEADS = 4
GLA_DK = BRANCH_WIDTH // (2 * GLA_HEADS)
GLA_DV = BRANCH_WIDTH // GLA_HEADS
GLA_LOWRANK = 16
GLA_TAU = 16.0
GLA_CHUNK = 64
LRU_BLOCKS = 8
LRU_BLOCK = BRANCH_WIDTH // LRU_BLOCKS
LRU_C = 8.0
RWKV_HEAD = 64
RWKV_HEADS = BRANCH_WIDTH // RWKV_HEAD
RWKV_W_LORA = 32
RWKV_A_LORA = 32
RWKV_V_LORA = 32
RWKV_G_LORA = 96
RWKV_GN_EPS = 64e-5
RWKV_SIZES = (BRANCH_WIDTH, BRANCH_WIDTH, BRANCH_WIDTH, RWKV_W_LORA, RWKV_A_LORA, RWKV_G_LORA)
RWKV_IN = sum(RWKV_SIZES)
IN_SIZES = (BRANCH_WIDTH, SSD_XBC, SSD_HEADS,
            GLA_HEADS * GLA_DK, GLA_HEADS * GLA_DK, BRANCH_WIDTH, BRANCH_WIDTH, GLA_LOWRANK,
            BRANCH_WIDTH, BRANCH_WIDTH,
            RWKV_IN,
            N_BRANCH * D_MODEL)
N_IN = sum(IN_SIZES)
FFN_HIDDEN = -(-8 * D_MODEL // (3 * 256)) * 256

kernel_name = 'hybrid_ssd_gla_rglru_rwkv7_trunk'


def _split(t, sizes):
    out = []
    start = 0
    for n in sizes:
        out.append(t[..., start:start + n])
        start += n
    return out


def _rms_stat(x):
    x = x.astype(jnp.float32)
    return x * lax.rsqrt(jnp.mean(x * x, axis=-1, keepdims=True) + RMS_EPS)


def _rms(x, g):
    return _rms_stat(x) * g.astype(jnp.float32)


def _shift(t):
    return jnp.pad(t, ((0, 0), (1, 0), (0, 0)))[:, :-1]


def _causal_dwconv(x, w, b):
    ch = x.shape[-1]
    y = lax.conv_general_dilated(x, w[:, None, :].astype(x.dtype), window_strides=(1,),
                                 padding=[(w.shape[0] - 1, 0)],
                                 dimension_numbers=('NWC', 'WIO', 'NWC'),
                                 feature_group_count=ch)
    return y + b.astype(y.dtype)


def _to_chunks(t, c):
    b, s = t.shape[:2]
    return jnp.moveaxis(t.reshape(b, s // c, c, *t.shape[2:]), 1, 0)


def _from_chunks(t):
    nc, b, c = t.shape[:3]
    return jnp.moveaxis(t, 0, 1).reshape(b, nc * c, *t.shape[3:])


def _ssd_chunked(xh, dt, a, bm, cm):
    c = SSD_CHUNK
    bsz = xh.shape[0]
    mask = jnp.tril(jnp.ones((c, c), bool))[None, :, :, None, None]

    def step(state, inp):
        xc, dtc, bc, cc = inp
        cum = jnp.cumsum(dtc * a, axis=1)
        seg = jnp.where(mask, cum[:, :, None] - cum[:, None, :], -jnp.inf)
        w = jnp.einsum('btgn,bsgn->btsg', cc, bc)[..., None] * jnp.exp(seg) * dtc[:, None]
        y = jnp.einsum('btsgr,bsgrp->btgrp', w, xc)
        y = y + jnp.einsum('btgn,bgrpn->btgrp', cc, state) * jnp.exp(cum)[..., None]
        decay = jnp.exp(cum[:, -1:] - cum) * dtc
        state = state * jnp.exp(cum[:, -1])[..., None, None] + jnp.einsum('bsgr,bsgn,bsgrp->bgrpn', decay, bc, xc)
        return state, y

    init = jnp.zeros((bsz, SSD_GROUPS, SSD_HPG, SSD_HEADDIM, SSD_STATE), jnp.float32)
    _, ys = lax.scan(step, init, (_to_chunks(xh, c), _to_chunks(dt, c), _to_chunks(bm, c), _to_chunks(cm, c)))
    return _from_chunks(ys)


def _ssd_mixer(z, xbc, dt_raw, conv_w, conv_b, dt_bias, a_log, d_skip, norm_g):
    bsz, seq, _ = z.shape
    f32 = jnp.float32
    xbc = jax.nn.silu(_causal_dwconv(xbc, conv_w, conv_b).astype(f32))
    xs, bm, cm = _split(xbc, (BRANCH_WIDTH, SSD_GROUPS * SSD_STATE, SSD_GROUPS * SSD_STATE))
    xh = xs.reshape(bsz, seq, SSD_GROUPS, SSD_HPG, SSD_HEADDIM)
    bm = bm.reshape(bsz, seq, SSD_GROUPS, SSD_STATE)
    cm = cm.reshape(bsz, seq, SSD_GROUPS, SSD_STATE)
    dt = jax.nn.softplus(dt_raw.astype(f32) + dt_bias.astype(f32)).reshape(bsz, seq, SSD_GROUPS, SSD_HPG)
    a = -jnp.exp(a_log.astype(f32)).reshape(SSD_GROUPS, SSD_HPG)
    y = _ssd_chunked(xh, dt, a, bm, cm) + d_skip.astype(f32).reshape(SSD_GROUPS, SSD_HPG, 1) * xh
    y = y.reshape(bsz, seq, BRANCH_WIDTH) * jax.nn.silu(z.astype(f32))
    y = _rms_stat(y.reshape(bsz, seq, SSD_GROUPS, BRANCH_WIDTH // SSD_GROUPS)).reshape(bsz, seq, BRANCH_WIDTH)
    return y * norm_g.astype(f32)


def _gla_chunked(q, k, v, log_a):
    c = GLA_CHUNK
    bsz = q.shape[0]
    mask = jnp.tril(jnp.ones((c, c), bool))

    def step(state, inp):
        qc, kc, vc, gc = inp
        b = jnp.cumsum(gc, axis=1)
        ref = b[:, c // 2:c // 2 + 1]
        att = jnp.einsum('bthd,bshd->bhts', qc * jnp.exp(b - ref), kc * jnp.exp(ref - b))
        att = jnp.where(mask, att, 0.0)
        y = jnp.einsum('bhts,bshv->bthv', att, vc) + jnp.einsum('bthd,bhdv->bthv', qc * jnp.exp(b), state)
        b_last = b[:, -1]
        state = state * jnp.exp(b_last)[..., None] + jnp.einsum('bshd,bshv->bhdv', kc * jnp.exp(b_last[:, None] - b), vc)
        return state, y

    init = jnp.zeros((bsz, GLA_HEADS, GLA_DK, GLA_DV), jnp.float32)
    _, ys = lax.scan(step, init, (_to_chunks(q, c), _to_chunks(k, c), _to_chunks(v, c), _to_chunks(log_a, c)))
    return _from_chunks(ys)


def _gla_mixer(q, k, v, r, a_lo, alpha_w, alpha_b, norm_g):
    bsz, seq, _ = q.shape
    f32 = jnp.float32
    q = q.astype(f32).reshape(bsz, seq, GLA_HEADS, GLA_DK) * (GLA_DK ** -0.5)
    k = k.astype(f32).reshape(bsz, seq, GLA_HEADS, GLA_DK)
    v = v.astype(f32).reshape(bsz, seq, GLA_HEADS, GLA_DV)
    log_a = jax.nn.log_sigmoid(a_lo.astype(f32) @ alpha_w.astype(f32) + alpha_b.astype(f32)) / GLA_TAU
    log_a = log_a.reshape(bsz, seq, GLA_HEADS, GLA_DK)
    o = _rms_stat(_gla_chunked(q, k, v, log_a)) * norm_g.astype(f32)
    return o.reshape(bsz, seq, BRANCH_WIDTH) * jax.nn.silu(r.astype(f32))


def _linear_combine(left, right):
    a_l, b_l = left
    a_r, b_r = right
    return (a_l * a_r, a_r * b_l + b_r)


def _rglru_mixer(xb, gate, conv_w, conv_b, wa, ba, wx, bx, lam):
    bsz, seq, _ = xb.shape
    f32 = jnp.float32
    xb = _causal_dwconv(xb, conv_w, conv_b).astype(f32)
    blk = xb.reshape(bsz, seq, LRU_BLOCKS, LRU_BLOCK)
    r = jax.nn.sigmoid(jnp.einsum('bshi,hij->bshj', blk, wa.astype(f32)).reshape(bsz, seq, BRANCH_WIDTH) + ba.astype(f32))
    i = jax.nn.sigmoid(jnp.einsum('bshi,hij->bshj', blk, wx.astype(f32)).reshape(bsz, seq, BRANCH_WIDTH) + bx.astype(f32))
    log_a = -LRU_C * r * jax.nn.softplus(-lam.astype(f32))
    a = jnp.exp(log_a)
    u = jnp.sqrt(-jnp.expm1(2.0 * log_a)) * (i * xb)
    _, hs = lax.associative_scan(_linear_combine, (a, u), axis=1)
    return hs * jax.nn.gelu(gate.astype(f32), approximate=True)


def _rwkv7_scan(r, w, k, v, aa, bb):
    bsz, seq, nh, n = r.shape

    def step(state, inp):
        r_t, w_t, k_t, v_t, a_t, b_t = inp
        sa = jnp.einsum('bhij,bhj->bhi', state, a_t)
        state = state * w_t[:, :, None, :] + sa[..., None] * b_t[:, :, None, :] + v_t[..., None] * k_t[:, :, None, :]
        return state, jnp.einsum('bhij,bhj->bhi', state, r_t)

    init = jnp.zeros((bsz, nh, n, n), jnp.float32)
    xs = (jnp.moveaxis(r, 1, 0), jnp.moveaxis(w, 1, 0), jnp.moveaxis(k, 1, 0),
          jnp.moveaxis(v, 1, 0), jnp.moveaxis(aa, 1, 0), jnp.moveaxis(bb, 1, 0))
    _, ys = lax.scan(step, init, xs)
    return jnp.moveaxis(ys, 0, 1)


def _rwkv7_mixer(p, vres_lo, v_first, mu, w0, w_w2, a0, a_w2, v0, v_w2, g_w2, k_k, k_a, r_k, ln_g, ln_b):
    bsz, seq, _ = p.shape
    f32 = jnp.float32
    hd = (bsz, seq, RWKV_HEADS, RWKV_HEAD)
    p = p.astype(f32)
    p = p + (_shift(p) - p) * mu.astype(f32)
    r, k, v, w_lo, a_lo, g_lo = _split(p, RWKV_SIZES)
    w_log = -jax.nn.softplus(-(w0.astype(f32) + jnp.tanh(w_lo) @ w_w2.astype(f32))) - 0.5
    decay = jnp.exp(-jnp.exp(w_log))
    if v_first is None:
        v_first = v
    else:
        v = v + (v_first - v) * jax.nn.sigmoid(v0.astype(f32) + vres_lo.astype(f32) @ v_w2.astype(f32))
    a = jax.nn.sigmoid(a0.astype(f32) + a_lo @ a_w2.astype(f32))
    g = jax.nn.sigmoid(g_lo) @ g_w2.astype(f32)
    kk = (k * k_k.astype(f32)).reshape(hd)
    kk = kk / jnp.maximum(jnp.linalg.norm(kk, axis=-1, keepdims=True), 1e-12)
    k = k * (1.0 + (a - 1.0) * k_a.astype(f32))
    rh, kh, vh = r.reshape(hd), k.reshape(hd), v.reshape(hd)
    y = _rwkv7_scan(rh, decay.reshape(hd), kh, vh, -kk, kk * a.reshape(hd))
    mean = jnp.mean(y, axis=-1, keepdims=True)
    var = jnp.mean(jnp.square(y - mean), axis=-1, keepdims=True)
    y = ((y - mean) * lax.rsqrt(var + RWKV_GN_EPS)).reshape(bsz, seq, BRANCH_WIDTH) * ln_g.astype(f32) + ln_b.astype(f32)
    bonus = jnp.sum(rh * kh * r_k.astype(f32), axis=-1, keepdims=True) * vh
    y = y + bonus.reshape(bsz, seq, BRANCH_WIDTH)
    return y * g, v_first


def setup_inputs(seed: int = 0) -> dict:
    key = jax.random.key(seed)
    ks = iter(jax.random.split(key, 64))
    L = DEPTH
    D = D_MODEL
    W = BRANCH_WIDTH
    f32 = jnp.float32

    def nrm(shape, scale):
        return jax.random.normal(next(ks), shape, f32) * scale

    def unif(shape, lo, hi):
        return jax.random.uniform(next(ks), shape, f32, lo, hi)

    x = nrm((BATCH, SEQ, D), 1.0)
    norm_g = 1.0 + nrm((L, 4, D), 0.05)
    w_in = nrm((L, D, N_IN), D ** -0.5)
    w_vres_in = nrm((L - 1, D, RWKV_V_LORA), D ** -0.5)
    ssd_conv_w = nrm((L, CONV_WIDTH, SSD_XBC), CONV_WIDTH ** -0.5)
    ssd_conv_b = nrm((L, SSD_XBC), 0.02)
    dt0 = jnp.exp(unif((L, SSD_HEADS), math.log(1e-3), math.log(1e-1)))
    ssd_dt_bias = dt0 + jnp.log(-jnp.expm1(-dt0))
    ssd_a_log = jnp.log(unif((L, SSD_HEADS), 1.0, 16.0))
    ssd_d = 1.0 + nrm((L, SSD_HEADS), 0.1)
    ssd_norm_g = 1.0 + nrm((L, W), 0.05)
    gla_alpha_w = nrm((L, GLA_LOWRANK, GLA_HEADS * GLA_DK), GLA_LOWRANK ** -0.5)
    gla_alpha_b = nrm((L, GLA_HEADS * GLA_DK), 0.1)
    gla_norm_g = 1.0 + nrm((L, GLA_DV), 0.05)
    lru_conv_w = nrm((L, CONV_WIDTH, W), CONV_WIDTH ** -0.5)
    lru_conv_b = nrm((L, W), 0.02)
    lru_wa = nrm((L, LRU_BLOCKS, LRU_BLOCK, LRU_BLOCK), LRU_BLOCK ** -0.5)
    lru_ba = nrm((L, W), 0.02)
    lru_wx = nrm((L, LRU_BLOCKS, LRU_BLOCK, LRU_BLOCK), LRU_BLOCK ** -0.5)
    lru_bx = nrm((L, W), 0.02)
    s = unif((L, W), 0.9, 0.999) ** (1.0 / LRU_C)
    lru_lambda = jnp.log(s) - jnp.log1p(-s)
    rwkv_mu = unif((L, RWKV_IN), 0.0, 1.0)
    rwkv_w0 = unif((L, W), -6.5, -1.5)
    rwkv_w_w2 = nrm((L, RWKV_W_LORA, W), 0.5 * RWKV_W_LORA ** -0.5)
    rwkv_a0 = nrm((L, W), 0.1)
    rwkv_a_w2 = nrm((L, RWKV_A_LORA, W), RWKV_A_LORA ** -0.5)
    rwkv_v0 = 1.0 + nrm((L - 1, W), 0.1)
    rwkv_v_w2 = nrm((L - 1, RWKV_V_LORA, W), RWKV_V_LORA ** -0.5)
    rwkv_g_w2 = nrm((L, RWKV_G_LORA, W), RWKV_G_LORA ** -0.5)
    rwkv_k_k = 0.85 + nrm((L, W), 0.02)
    rwkv_k_a = 1.0 + nrm((L, W), 0.02)
    rwkv_r_k = -0.04 + nrm((L, RWKV_HEADS, RWKV_HEAD), 0.02)
    rwkv_ln_g = 1.0 + nrm((L, W), 0.05)
    rwkv_ln_b = nrm((L, W), 0.02)
    w_branch = nrm((L, N_BRANCH, W, D), W ** -0.5)
    w_out = nrm((L, D, D), D ** -0.5)
    ffn_w_gate = nrm((L, D, FFN_HIDDEN), D ** -0.5)
    ffn_w_up = nrm((L, D, FFN_HIDDEN), D ** -0.5)
    ffn_w_down = nrm((L, FFN_HIDDEN, D), FFN_HIDDEN ** -0.5)
    return {'x': x, 'norm_g': norm_g, 'w_in': w_in, 'w_vres_in': w_vres_in,
            'ssd_conv_w': ssd_conv_w, 'ssd_conv_b': ssd_conv_b, 'ssd_dt_bias': ssd_dt_bias,
            'ssd_a_log': ssd_a_log, 'ssd_d': ssd_d, 'ssd_norm_g': ssd_norm_g,
            'gla_alpha_w': gla_alpha_w, 'gla_alpha_b': gla_alpha_b, 'gla_norm_g': gla_norm_g,
            'lru_conv_w': lru_conv_w, 'lru_conv_b': lru_conv_b, 'lru_wa': lru_wa, 'lru_ba': lru_ba,
            'lru_wx': lru_wx, 'lru_bx': lru_bx, 'lru_lambda': lru_lambda,
            'rwkv_mu': rwkv_mu, 'rwkv_w0': rwkv_w0, 'rwkv_w_w2': rwkv_w_w2, 'rwkv_a0': rwkv_a0,
            'rwkv_a_w2': rwkv_a_w2, 'rwkv_v0': rwkv_v0, 'rwkv_v_w2': rwkv_v_w2, 'rwkv_g_w2': rwkv_g_w2,
            'rwkv_k_k': rwkv_k_k, 'rwkv_k_a': rwkv_k_a, 'rwkv_r_k': rwkv_r_k,
            'rwkv_ln_g': rwkv_ln_g, 'rwkv_ln_b': rwkv_ln_b,
            'w_branch': w_branch, 'w_out': w_out,
            'ffn_w_gate': ffn_w_gate, 'ffn_w_up': ffn_w_up, 'ffn_w_down': ffn_w_down}


def reference(x, norm_g, w_in, w_vres_in, ssd_conv_w, ssd_conv_b, ssd_dt_bias, ssd_a_log, ssd_d, ssd_norm_g,
              gla_alpha_w, gla_alpha_b, gla_norm_g, lru_conv_w, lru_conv_b, lru_wa, lru_ba, lru_wx, lru_bx,
              lru_lambda, rwkv_mu, rwkv_w0, rwkv_w_w2, rwkv_a0, rwkv_a_w2, rwkv_v0, rwkv_v_w2, rwkv_g_w2,
              rwkv_k_k, rwkv_k_a, rwkv_r_k, rwkv_ln_g, rwkv_ln_b, w_branch, w_out,
              ffn_w_gate, ffn_w_up, ffn_w_down):
    bsz, seq, _ = x.shape
    cdt = x.dtype
    v_first = None
    for layer in range(DEPTH):
        h = _rms(x, norm_g[layer, 0]).astype(cdt)
        if layer == 0:
            proj = h @ w_in[layer]
            vres_lo = None
            v0 = None
            v_w2 = None
        else:
            proj = h @ jnp.concatenate([w_in[layer], w_vres_in[layer - 1]], axis=1)
            vres_lo = proj[..., N_IN:]
            v0 = rwkv_v0[layer - 1]
            v_w2 = rwkv_v_w2[layer - 1]
        (ssd_z, ssd_xbc, ssd_dt, gla_q, gla_k, gla_v, gla_r, gla_lo,
         lru_x, lru_gate, rwkv_p, gate_logits) = _split(proj, IN_SIZES)
        y_ssd = _ssd_mixer(ssd_z, ssd_xbc, ssd_dt, ssd_conv_w[layer], ssd_conv_b[layer], ssd_dt_bias[layer],
                           ssd_a_log[layer], ssd_d[layer], ssd_norm_g[layer])
        y_gla = _gla_mixer(gla_q, gla_k, gla_v, gla_r, gla_lo, gla_alpha_w[layer], gla_alpha_b[layer], gla_norm_g[layer])
        y_lru = _rglru_mixer(lru_x, lru_gate, lru_conv_w[layer], lru_conv_b[layer], lru_wa[layer], lru_ba[layer],
                             lru_wx[layer], lru_bx[layer], lru_lambda[layer])
        y_rwkv, v_first = _rwkv7_mixer(rwkv_p, vres_lo, v_first, rwkv_mu[layer], rwkv_w0[layer], rwkv_w_w2[layer],
                                       rwkv_a0[layer], rwkv_a_w2[layer], v0, v_w2, rwkv_g_w2[layer],
                                       rwkv_k_k[layer], rwkv_k_a[layer], rwkv_r_k[layer],
                                       rwkv_ln_g[layer], rwkv_ln_b[layer])
        y_cat = jnp.stack([y_ssd, y_gla, y_lru, y_rwkv], axis=2).astype(cdt)
        branch = jnp.einsum('bskc,kcd->bskd', y_cat, w_branch[layer])
        gates = jax.nn.sigmoid(gate_logits.astype(jnp.float32).reshape(bsz, seq, N_BRANCH, D_MODEL))
        merged = jnp.sum(gates * branch.astype(jnp.float32), axis=2).astype(cdt)
        o = merged @ w_out[layer]
        x = x + _rms(o, norm_g[layer, 1]).astype(cdt)
        h2 = _rms(x, norm_g[layer, 2]).astype(cdt)
        f = (jax.nn.silu(h2 @ ffn_w_gate[layer]) * (h2 @ ffn_w_up[layer])) @ ffn_w_down[layer]
        x = x + _rms(f, norm_g[layer, 3]).astype(cdt)
    return x
```

```python
import functools

import jax
import jax.numpy as jnp
from jax import lax
from jax.experimental import pallas as pl
from jax.experimental.pallas import tpu as pltpu

F32 = jnp.float32
BF16 = jnp.bfloat16

D_MODEL = 1024
WIDTH = 512
RMS_EPS = 1e-6
CONV_WIDTH = 4
SSD_HEADS = 8
SSD_HEADDIM = 64
SSD_STATE = 64
SSD_CHUNK = 128
SSD_XBC = WIDTH + 4 * SSD_STATE
GLA_HEADS = 4
GLA_DK = 64
GLA_DV = 128
GLA_LOWRANK = 16
GLA_TAU = 16.0
GLA_CHUNK = 64
LRU_C = 8.0
RWKV_HEADS = 8
RWKV_HEAD = 64
RWKV_CHUNK = 64
RWKV_GN_EPS = 64e-5
RWKV_LORAS = (32, 32, 96)
RWKV_V_LORA = 32
FFN_HIDDEN = 2816
FFN_SPLIT = 2
LANE = 128
SUBLANE = 8
TIME_TILE = 512
TOKEN_TILE = 512
VMEM_LIMIT_BYTES = 56 * 1024 * 1024

IN_SIZES = (WIDTH, SSD_XBC, SSD_HEADS,
            GLA_HEADS * GLA_DK, GLA_HEADS * GLA_DK, WIDTH, WIDTH, GLA_LOWRANK,
            WIDTH, WIDTH,
            3 * WIDTH + sum(RWKV_LORAS),
            4 * D_MODEL)
N_IN = sum(IN_SIZES)


def _rmsn(x):
    return x * lax.rsqrt(jnp.mean(x * x, axis=-1, keepdims=True) + RMS_EPS)


def _dot(a, b):
    return jnp.dot(a.astype(BF16), b.astype(BF16), preferred_element_type=F32)


def _dot_nt(a, b):
    return lax.dot_general(a.astype(BF16), b.astype(BF16), (((1,), (1,)), ((), ())),
                           preferred_element_type=F32)


def _dot_tn(a, b):
    return lax.dot_general(a.astype(BF16), b.astype(BF16), (((0,), (0,)), ((), ())),
                           preferred_element_type=F32)


def _split3(x):
    x1 = x.astype(BF16)
    r1 = x - x1.astype(F32)
    x2 = r1.astype(BF16)
    x3 = (r1 - x2.astype(F32)).astype(BF16)
    return x1, x2, x3


def _sel_dot(sel, x):
    s = sel.astype(BF16)
    return sum(jnp.dot(s, p, preferred_element_type=F32) for p in _split3(x))


def _dot_sel(x, sel):
    s = sel.astype(BF16)
    return sum(jnp.dot(p, s, preferred_element_type=F32) for p in _split3(x))


def _iota(shape, axis):
    return lax.broadcasted_iota(jnp.int32, shape, axis)


def _tril(n, m=None, strict=False):
    m = n if m is None else m
    r, c = _iota((n, m), 0), _iota((n, m), 1)
    return (r > c) if strict else (r >= c)


def _shift_rows(x, tail, j):
    r = pltpu.roll(x, j, 0)
    pr = pltpu.roll(tail, j, 0)
    rid = _iota(tail.shape, 0)
    head = jnp.where(rid < j, pr, r[:SUBLANE])
    return jnp.concatenate([head, r[SUBLANE:]], axis=0)


def _causal_conv(x, tail, w, b):
    y = x * w[CONV_WIDTH - 1:CONV_WIDTH] + b
    for j in range(1, CONV_WIDTH):
        y = y + _shift_rows(x, tail, j) * w[CONV_WIDTH - 1 - j:CONV_WIDTH - j]
    return y


def _expand_heads(v, n_heads, width):
    rows = v.shape[0]
    return jnp.concatenate(
        [jnp.broadcast_to(v[:, h:h + 1], (rows, width)) for h in range(n_heads)], axis=1)


def _sigmoid(x):
    return jax.nn.sigmoid(x)


def _silu(x):
    return x * jax.nn.sigmoid(x)


def _normed_input(x_ref, g_ref):
    return (_rmsn(x_ref[0]) * g_ref[...]).astype(BF16)


def _ssd_kernel(x_ref, g_ref, wz_ref, wxbc_ref, wdt_ref, cw_ref, cb_ref, dtb_ref, alog_ref,
                dsk_ref, ng_ref, o_ref, tail_ref, st_ref, *, tt):
    c_len = SSD_CHUNK
    n_st = 2 * SSD_STATE

    @pl.when(pl.program_id(1) == 0)
    def _():
        tail_ref[...] = jnp.zeros_like(tail_ref)
        st_ref[...] = jnp.zeros_like(st_ref)

    h = _normed_input(x_ref, g_ref)
    z = jnp.dot(h, wz_ref[...], preferred_element_type=F32)
    xbc_raw = jnp.dot(h, wxbc_ref[...], preferred_element_type=F32)
    dt_raw = jnp.dot(h, wdt_ref[...], preferred_element_type=F32)
    xbc = _silu(_causal_conv(xbc_raw, tail_ref[...], cw_ref[...], cb_ref[...]))
    tail_ref[...] = xbc_raw[tt - SUBLANE:]
    dt = jax.nn.softplus(dt_raw + dtb_ref[...])
    da = dt * (-jnp.exp(alog_ref[...]))

    ltri = _tril(c_len).astype(F32)
    tril = _tril(c_len)
    lane = _iota((c_len, LANE), 1)
    low = lane < SSD_HEADDIM
    st_mask = (_iota((n_st, WIDTH), 0) < SSD_STATE) == (_iota((n_st, WIDTH), 1) < WIDTH // 2)
    st = st_ref[...]
    ys = []
    for c in range(tt // c_len):
        sl = slice(c * c_len, (c + 1) * c_len)
        xs = xbc[sl, :WIDTH]
        bm = xbc[sl, WIDTH:WIDTH + n_st]
        cm = xbc[sl, WIDTH + n_st:]
        dtc = dt[sl]
        cum = _sel_dot(ltri, da[sl])
        cum_t = cum.T
        dt_t = dtc.T
        cb = (_dot_nt(jnp.where(low, cm, 0.0), bm), _dot_nt(jnp.where(low, 0.0, cm), bm))
        pairs = []
        for hp in range(SSD_HEADS // 2):
            cbg = cb[hp // 2]
            ws = []
            for hd in (2 * hp, 2 * hp + 1):
                seg = cum[:, hd:hd + 1] - cum_t[hd:hd + 1, :]
                m = jnp.exp(jnp.where(tril, seg, -jnp.inf)) * dt_t[hd:hd + 1, :]
                ws.append((cbg * m).astype(BF16))
            xp = xs[:, hp * LANE:(hp + 1) * LANE]
            rhs = jnp.concatenate([jnp.where(low, xp, 0.0), jnp.where(low, 0.0, xp)], axis=0)
            pairs.append(_dot(jnp.concatenate(ws, axis=1), rhs))
        y = jnp.concatenate(pairs, axis=1)
        y = y + _dot(cm, st) * _expand_heads(jnp.exp(cum), SSD_HEADS, SSD_HEADDIM)
        ys.append(y + dsk_ref[...] * xs)
        cum_last = cum[c_len - 1:c_len]
        decay = jnp.exp(cum_last - cum) * dtc
        upd = _dot_tn(bm, xs * _expand_heads(decay, SSD_HEADS, SSD_HEADDIM))
        st = (st * _expand_heads(jnp.exp(cum_last), SSD_HEADS, SSD_HEADDIM)
              + jnp.where(st_mask, upd, 0.0))
    st_ref[...] = st
    y = jnp.concatenate(ys, axis=0) * _silu(z)
    half = WIDTH // 2
    y = jnp.concatenate([_rmsn(y[:, :half]), _rmsn(y[:, half:])], axis=1)
    o_ref[0] = (y * ng_ref[...]).astype(o_ref.dtype)


def _gla_kernel(x_ref, g_ref, wq_ref, wk_ref, wv_ref, wr_ref, wlo_ref, aw_ref, ab_ref, ng_ref,
                o_ref, st_ref, *, tt):
    c_len = GLA_CHUNK
    qk = GLA_HEADS * GLA_DK

    @pl.when(pl.program_id(1) == 0)
    def _():
        st_ref[...] = jnp.zeros_like(st_ref)

    h = _normed_input(x_ref, g_ref)
    q = jnp.dot(h, wq_ref[...], preferred_element_type=F32) * (GLA_DK ** -0.5)
    k = jnp.dot(h, wk_ref[...], preferred_element_type=F32)
    v = jnp.dot(h, wv_ref[...], preferred_element_type=F32)
    r = jnp.dot(h, wr_ref[...], preferred_element_type=F32)
    lo = jnp.dot(h, wlo_ref[...], preferred_element_type=F32)
    log_a = jax.nn.log_sigmoid(_dot(lo, aw_ref[...]) + ab_ref[...]) / GLA_TAU

    ltri = _tril(c_len).astype(F32)
    tril_stack = (_iota((GLA_HEADS * c_len, c_len), 0) & (c_len - 1)) >= _iota((GLA_HEADS * c_len, c_len), 1)
    head_of_lane = _iota((c_len, qk), 1) >> 6
    head_of_lane_dv = _iota((GLA_DV, qk), 1) >> 6
    st = st_ref[...]
    ys = []
    for c in range(tt // c_len):
        sl = slice(c * c_len, (c + 1) * c_len)
        b = _sel_dot(ltri, log_a[sl])
        ref = b[c_len // 2:c_len // 2 + 1]
        b_last = b[c_len - 1:c_len]
        qc, kc, vc = q[sl], k[sl], v[sl]
        qe = qc * jnp.exp(b - ref)
        ke = kc * jnp.exp(ref - b)
        qb = qc * jnp.exp(b)
        kl = kc * jnp.exp(b_last - b)
        stack = lambda a: jnp.concatenate(
            [jnp.where(head_of_lane == hd, a, 0.0) for hd in range(GLA_HEADS)], axis=0)
        att = jnp.where(tril_stack, _dot_nt(stack(qe), ke), 0.0)
        inter = _dot_nt(stack(qb), st)
        outs = []
        for hd in range(GLA_HEADS):
            rows = slice(hd * c_len, (hd + 1) * c_len)
            outs.append(_dot(att[rows], vc[:, hd * GLA_DV:(hd + 1) * GLA_DV]) + inter[rows])
        ys.append(jnp.concatenate(outs, axis=1))
        full = _dot_tn(vc, kl)
        upd = sum(jnp.where(head_of_lane_dv == hd, full[hd * GLA_DV:(hd + 1) * GLA_DV], 0.0)
                  for hd in range(GLA_HEADS))
        st = st * jnp.exp(b_last) + upd
    st_ref[...] = st
    y = jnp.concatenate(ys, axis=0)
    o = jnp.concatenate(
        [_rmsn(y[:, hd * GLA_DV:(hd + 1) * GLA_DV]) * ng_ref[...] for hd in range(GLA_HEADS)], axis=1)
    o_ref[0] = (o * _silu(r)).astype(o_ref.dtype)


def _lru_kernel(x_ref, g_ref, wx_ref, wg_ref, cw_ref, cb_ref, wa_ref, ba_ref, wi_ref, bi_ref,
                lam_ref, o_ref, tail_ref, carry_ref, *, tt):
    @pl.when(pl.program_id(1) == 0)
    def _():
        tail_ref[...] = jnp.zeros_like(tail_ref)
        carry_ref[...] = jnp.zeros_like(carry_ref)

    h = _normed_input(x_ref, g_ref)
    xr = jnp.dot(h, wx_ref[...], preferred_element_type=F32)
    gate = jnp.dot(h, wg_ref[...], preferred_element_type=F32)
    xb = _causal_conv(xr, tail_ref[...], cw_ref[...], cb_ref[...])
    tail_ref[...] = xr[tt - SUBLANE:]
    r = _sigmoid(_dot(xb, wa_ref[...]) + ba_ref[...])
    i = _sigmoid(_dot(xb, wi_ref[...]) + bi_ref[...])
    log_a = -LRU_C * r * jax.nn.softplus(-lam_ref[...])
    a = jnp.exp(log_a)
    u = jnp.sqrt(1.0 - jnp.exp(2.0 * log_a)) * (i * xb)

    rid = _iota((tt, WIDTH), 0) & (SUBLANE - 1)
    s = 1
    while s < SUBLANE:
        m = rid >= s
        u = jnp.where(m, a * pltpu.roll(u, s, 0) + u, u)
        a = jnp.where(m, a * pltpu.roll(a, s, 0), a)
        s *= 2
    carry = carry_ref[...]
    blocks = []
    for blk in range(tt // SUBLANE):
        sl = slice(blk * SUBLANE, (blk + 1) * SUBLANE)
        hb = u[sl] + a[sl] * carry
        blocks.append(hb)
        carry = jnp.broadcast_to(hb[SUBLANE - 1:], (SUBLANE, WIDTH))
    carry_ref[...] = carry
    hs = jnp.concatenate(blocks, axis=0)
    o_ref[0] = (hs * jax.nn.gelu(gate, approximate=True)).astype(o_ref.dtype)


def _rwkv_kernel(*refs, tt, has_vres):
    (x_ref, g_ref, wr_ref, wk_ref, wv_ref, wlo_ref, mur_ref, muk_ref, muv_ref, mulo_ref,
     w0_ref, ww2_ref, a0_ref, aw2_ref, gw2_ref, kk_ref, ka_ref, rk_ref, lng_ref, lnb_ref,
     hsum_ref) = refs[:21]
    if has_vres:
        v0_ref, vw2_ref, vf_ref, o_ref = refs[21:25]
        scratch = refs[25:]
    else:
        o_ref, vfo_ref = refs[21:23]
        scratch = refs[23:]
    (tr_ref, tk_ref, tv_ref, tlo_ref, s_ref, r_s, a_s, b_s, k_s, v_s, w_s, y_s) = scratch
    c_len = RWKV_CHUNK
    n = RWKV_HEAD

    @pl.when(pl.program_id(1) == 0)
    def _():
        for ref in (tr_ref, tk_ref, tv_ref, tlo_ref, s_ref):
            ref[...] = jnp.zeros_like(ref)

    h = _normed_input(x_ref, g_ref)

    def shifted(w_ref, tail_ref, mu_ref):
        p = jnp.dot(h, w_ref[...], preferred_element_type=F32)
        prev = _shift_rows(p, tail_ref[...], 1)
        tail_ref[...] = p[tt - SUBLANE:]
        return p + (prev - p) * mu_ref[...]

    r = shifted(wr_ref, tr_ref, mur_ref)
    k = shifted(wk_ref, tk_ref, muk_ref)
    v = shifted(wv_ref, tv_ref, muv_ref)
    lo = shifted(wlo_ref, tlo_ref, mulo_ref)
    w_lo, a_lo, g_lo = lo[:, :LANE], lo[:, LANE:2 * LANE], lo[:, 2 * LANE:3 * LANE]
    w_log = -jax.nn.softplus(-(w0_ref[...] + _dot(jnp.tanh(w_lo), ww2_ref[...]))) - 0.5
    log_w = -jnp.exp(w_log)
    a = _sigmoid(a0_ref[...] + _dot(a_lo, aw2_ref[...]))
    g = _dot(_sigmoid(g_lo), gw2_ref[...])
    if has_vres:
        v = v + (vf_ref[0] - v) * _sigmoid(v0_ref[...] + _dot(lo[:, 3 * LANE:], vw2_ref[...]))
    else:
        vfo_ref[0] = v
    hsum = hsum_ref[...]
    kk = k * kk_ref[...]
    kk = kk / jnp.maximum(jnp.sqrt(_dot_sel(kk * kk, hsum)), 1e-12)
    k = k * (1.0 + (a - 1.0) * ka_ref[...])
    bonus = _dot_sel(r * k * rk_ref[...], hsum) * v
    r_s[...] = r
    a_s[...] = -kk
    b_s[...] = kk * a
    k_s[...] = k
    v_s[...] = v
    w_s[...] = log_w

    ltri = _tril(c_len).astype(F32)
    tril = _tril(c_len)
    stril = _tril(c_len, strict=True)
    eye = (_iota((c_len, c_len), 0) == _iota((c_len, c_len), 1)).astype(F32)

    def chunk(c, carry):
        rows = pl.ds(pl.multiple_of(c * c_len, c_len), c_len)
        lw = w_s[rows, :]
        lc = _sel_dot(ltri, lw)
        e_pos = jnp.exp(lc)
        e_neg = jnp.exp(-lc)
        rt = r_s[rows, :] * e_pos
        at = a_s[rows, :] * jnp.exp(lc - lw)
        bt = b_s[rows, :] * e_neg
        kt = k_s[rows, :] * e_neg
        vv = v_s[rows, :]
        e_last = e_pos[c_len - 1:c_len]
        state = s_ref[...]
        for hd in range(RWKV_HEADS):
            ls = slice(hd * n, (hd + 1) * n)
            a_h, r_h, b_h, k_h, v_h, s_h = at[:, ls], rt[:, ls], bt[:, ls], kt[:, ls], vv[:, ls], state[:, ls]
            bk = jnp.concatenate([b_h, k_h], axis=0)
            m = _dot_nt(jnp.concatenate([a_h, r_h], axis=0), bk)
            l_ab = jnp.where(stril, m[:c_len, :c_len], 0.0)
            l_ak = jnp.where(stril, m[:c_len, c_len:], 0.0)
            a_rb = jnp.where(tril, m[c_len:, :c_len], 0.0)
            a_rk = jnp.where(tril, m[c_len:, c_len:], 0.0)
            inv = eye + l_ab
            p = l_ab
            span = 2
            while span < c_len:
                p = _dot(p, p)
                inv = inv + _dot(inv, p)
                span *= 2
            tz = _dot(inv, jnp.concatenate([a_h, _dot(l_ak, v_h)], axis=1))
            gs = _dot_nt(jnp.concatenate([tz[:, :n], r_h], axis=0), s_h)
            u = gs[:c_len] + tz[:, n:]
            uv = jnp.concatenate([u, v_h], axis=0)
            y_s[rows, ls] = gs[c_len:] + _dot(jnp.concatenate([a_rb, a_rk], axis=1), uv)
            s_ref[:, ls] = (s_h + _dot_tn(uv, bk)) * e_last[:, ls]
        return carry

    lax.fori_loop(0, tt // c_len, chunk, 0)

    y = y_s[...]
    mean = _dot_sel(y, hsum) * (1.0 / n)
    d = y - mean
    var = _dot_sel(d * d, hsum) * (1.0 / n)
    y = d * lax.rsqrt(var + RWKV_GN_EPS) * lng_ref[...] + lnb_ref[...] + bonus
    o_ref[0] = (y * g).astype(o_ref.dtype)


def _merge_kernel(x_ref, y0_ref, y1_ref, y2_ref, y3_ref, g0_ref, g1_ref, wgate_ref, wbr_ref, wout_ref,
                  o_ref):
    x = x_ref[...]
    h = (_rmsn(x) * g0_ref[...]).astype(BF16)
    acc = jnp.zeros(x.shape, F32)
    for kb, y_ref in enumerate((y0_ref, y1_ref, y2_ref, y3_ref)):
        logits = jnp.dot(h, wgate_ref[:, kb * D_MODEL:(kb + 1) * D_MODEL], preferred_element_type=F32)
        branch = jnp.dot(y_ref[...], wbr_ref[kb], preferred_element_type=F32)
        acc = acc + _sigmoid(logits) * branch
    o = jnp.dot(acc.astype(BF16), wout_ref[...], preferred_element_type=F32)
    o_ref[...] = x + _rmsn(o) * g1_ref[...]


def _ffn_kernel(x_ref, g2_ref, g3_ref, wg_ref, wu_ref, wd_ref, o_ref):
    x = x_ref[...]
    h = (_rmsn(x) * g2_ref[...]).astype(BF16)
    f = jnp.zeros(x.shape, F32)
    step = FFN_HIDDEN // FFN_SPLIT
    for c in range(FFN_SPLIT):
        cols = slice(c * step, (c + 1) * step)
        gt = jnp.dot(h, wg_ref[:, cols], preferred_element_type=F32)
        up = jnp.dot(h, wu_ref[:, cols], preferred_element_type=F32)
        f = f + jnp.dot((_silu(gt) * up).astype(BF16), wd_ref[cols, :], preferred_element_type=F32)
    o_ref[...] = x + _rmsn(f) * g3_ref[...]


def _const_spec(a):
    nd = a.ndim
    return pl.BlockSpec(a.shape, lambda *_: (0,) * nd, pipeline_mode=pl.Buffered(1))


def _time_tile(seq):
    return min(TIME_TILE, seq)


def _mixer_call(body, name, x, consts, extra_in=(), n_out_f32=0, scratch=(), **static):
    bsz, seq, _ = x.shape
    tt = _time_tile(seq)
    assert seq % tt == 0 and tt % SSD_CHUNK == 0
    tile = lambda width: pl.BlockSpec((1, tt, width), lambda b, t: (b, t, 0))
    in_specs = [tile(D_MODEL)] + [_const_spec(c) for c in consts] + [tile(e.shape[-1]) for e in extra_in]
    out_shape = [jax.ShapeDtypeStruct((bsz, seq, WIDTH), BF16)]
    out_shape += [jax.ShapeDtypeStruct((bsz, seq, WIDTH), F32)] * n_out_f32
    out_specs = [tile(WIDTH)] * len(out_shape)
    return pl.pallas_call(
        functools.partial(body, tt=tt, **static),
        grid=(bsz, seq // tt),
        in_specs=in_specs,
        out_specs=out_specs,
        out_shape=out_shape,
        scratch_shapes=list(scratch),
        compiler_params=pltpu.CompilerParams(
            dimension_semantics=("parallel", "arbitrary"), vmem_limit_bytes=VMEM_LIMIT_BYTES),
        name=name,
    )(x, *consts, *extra_in)


def _token_call(body, name, x2d, tiles, consts):
    tokens = x2d.shape[0]
    tm = min(TOKEN_TILE, tokens)
    assert tokens % tm == 0
    tile = lambda width: pl.BlockSpec((tm, width), lambda i: (i, 0))
    return pl.pallas_call(
        body,
        grid=(tokens // tm,),
        in_specs=[tile(D_MODEL)] + [tile(t.shape[-1]) for t in tiles] + [_const_spec(c) for c in consts],
        out_specs=tile(D_MODEL),
        out_shape=jax.ShapeDtypeStruct(x2d.shape, F32),
        compiler_params=pltpu.CompilerParams(
            dimension_semantics=("parallel",), vmem_limit_bytes=VMEM_LIMIT_BYTES),
        name=name,
    )(x2d, *tiles, *consts)


def _row(v, width=None):
    v = v.reshape(1, -1).astype(F32)
    if width is not None and v.shape[1] < width:
        v = jnp.pad(v, ((0, 0), (0, width - v.shape[1])))
    return v


def _pad_to(w, rows=None, cols=None):
    rows = w.shape[0] if rows is None else rows
    cols = w.shape[1] if cols is None else cols
    return jnp.pad(w, ((0, rows - w.shape[0]), (0, cols - w.shape[1])))


def _split_cols(w, sizes):
    out, start = [], 0
    for nsz in sizes:
        out.append(w[:, start:start + nsz])
        start += nsz
    return out


def _block_diag(w):
    nb, bi, bj = w.shape
    eye = jnp.eye(nb, dtype=w.dtype)
    return jnp.einsum('hij,hg->higj', w, eye).reshape(nb * bi, nb * bj)


def _ssd_mixer(x, p, layer, g0, w_z, w_xbc, w_dt):
    vmem = lambda *shape: pltpu.VMEM(shape, F32)
    y, = _mixer_call(
        _ssd_kernel, f"ssd_{layer}", x,
        [g0, w_z.astype(BF16), w_xbc.astype(BF16), _pad_to(w_dt, cols=LANE).astype(BF16),
         p['ssd_conv_w'][layer], _row(p['ssd_conv_b'][layer]), _row(p['ssd_dt_bias'][layer], LANE),
         _row(p['ssd_a_log'][layer], LANE), _row(jnp.repeat(p['ssd_d'][layer], SSD_HEADDIM)),
         _row(p['ssd_norm_g'][layer])],
        scratch=[vmem(SUBLANE, SSD_XBC), vmem(2 * SSD_STATE, WIDTH)])
    return y


def _gla_mixer(x, p, layer, g0, w_q, w_k, w_v, w_r, w_lo):
    y, = _mixer_call(
        _gla_kernel, f"gla_{layer}", x,
        [g0, w_q.astype(BF16), w_k.astype(BF16), w_v.astype(BF16), w_r.astype(BF16),
         _pad_to(w_lo, cols=LANE).astype(BF16), _pad_to(p['gla_alpha_w'][layer], rows=LANE).astype(BF16),
         _row(p['gla_alpha_b'][layer]), _row(p['gla_norm_g'][layer])],
        scratch=[pltpu.VMEM((GLA_DV, GLA_HEADS * GLA_DK), F32)])
    return y


def _lru_mixer(x, p, layer, g0, w_x, w_gate):
    y, = _mixer_call(
        _lru_kernel, f"lru_{layer}", x,
        [g0, w_x.astype(BF16), w_gate.astype(BF16), p['lru_conv_w'][layer], _row(p['lru_conv_b'][layer]),
         _block_diag(p['lru_wa'][layer]).astype(BF16), _row(p['lru_ba'][layer]),
         _block_diag(p['lru_wx'][layer]).astype(BF16), _row(p['lru_bx'][layer]),
         _row(p['lru_lambda'][layer])],
        scratch=[pltpu.VMEM((SUBLANE, WIDTH), F32)] * 2)
    return y


def _rwkv_mixer(x, p, layer, g0, w_rw, v_first):
    bf = lambda w: w.astype(BF16)
    tt = _time_tile(x.shape[1])
    sizes = (WIDTH, WIDTH, WIDTH) + RWKV_LORAS
    w_r, w_k, w_v, w_wlo, w_alo, w_glo = _split_cols(w_rw, sizes)
    mu_r, mu_k, mu_v, mu_wlo, mu_alo, mu_glo = _split_cols(p['rwkv_mu'][layer].reshape(1, -1), sizes)
    has_vres = v_first is not None
    w_vlo = p['w_vres_in'][layer - 1] if has_vres else jnp.zeros((D_MODEL, RWKV_V_LORA), F32)
    lo_w = jnp.concatenate([_pad_to(w, cols=LANE) for w in (w_wlo, w_alo, w_glo, w_vlo)], axis=1)
    lo_mu = jnp.concatenate([_pad_to(m, cols=LANE) for m in (mu_wlo, mu_alo, mu_glo)]
                            + [jnp.zeros((1, LANE), F32)], axis=1)
    head_sum = (jnp.arange(WIDTH)[:, None] // RWKV_HEAD == jnp.arange(WIDTH)[None, :] // RWKV_HEAD)
    consts = [g0, bf(w_r), bf(w_k), bf(w_v), bf(lo_w), mu_r, mu_k, mu_v, lo_mu,
              _row(p['rwkv_w0'][layer]), bf(_pad_to(p['rwkv_w_w2'][layer], rows=LANE)),
              _row(p['rwkv_a0'][layer]), bf(_pad_to(p['rwkv_a_w2'][layer], rows=LANE)),
              bf(_pad_to(p['rwkv_g_w2'][layer], rows=LANE)),
              _row(p['rwkv_k_k'][layer]), _row(p['rwkv_k_a'][layer]), _row(p['rwkv_r_k'][layer]),
              _row(p['rwkv_ln_g'][layer]), _row(p['rwkv_ln_b'][layer]), bf(head_sum)]
    vmem = lambda *shape: pltpu.VMEM(shape, F32)
    scratch = [vmem(SUBLANE, WIDTH)] * 4 + [vmem(RWKV_HEAD, WIDTH)] + [vmem(tt, WIDTH)] * 7
    if has_vres:
        consts += [_row(p['rwkv_v0'][layer - 1]), bf(_pad_to(p['rwkv_v_w2'][layer - 1], rows=LANE))]
        y, = _mixer_call(_rwkv_kernel, f"rwkv_{layer}", x, consts, extra_in=[v_first],
                         scratch=scratch, has_vres=True)
        return y, v_first
    return tuple(_mixer_call(_rwkv_kernel, f"rwkv_{layer}", x, consts, n_out_f32=1,
                             scratch=scratch, has_vres=False))


def _merge(x2d, ys, p, layer, g0, w_gates):
    return _token_call(
        _merge_kernel, f"merge_{layer}", x2d, ys,
        [g0, _row(p['norm_g'][layer, 1]), w_gates.astype(BF16), p['w_branch'][layer].astype(BF16),
         p['w_out'][layer].astype(BF16)])


def _ffn(x2d, p, layer):
    return _token_call(
        _ffn_kernel, f"ffn_{layer}", x2d, [],
        [_row(p['norm_g'][layer, 2]), _row(p['norm_g'][layer, 3]), p['ffn_w_gate'][layer].astype(BF16),
         p['ffn_w_up'][layer].astype(BF16), p['ffn_w_down'][layer].astype(BF16)])


def kernel(x, norm_g, w_in, w_vres_in, ssd_conv_w, ssd_conv_b, ssd_dt_bias, ssd_a_log, ssd_d, ssd_norm_g,
           gla_alpha_w, gla_alpha_b, gla_norm_g, lru_conv_w, lru_conv_b, lru_wa, lru_ba, lru_wx, lru_bx,
           lru_lambda, rwkv_mu, rwkv_w0, rwkv_w_w2, rwkv_a0, rwkv_a_w2, rwkv_v0, rwkv_v_w2, rwkv_g_w2,
           rwkv_k_k, rwkv_k_a, rwkv_r_k, rwkv_ln_g, rwkv_ln_b, w_branch, w_out,
           ffn_w_gate, ffn_w_up, ffn_w_down):
    p = dict(locals())
    bsz, seq, _ = x.shape
    flat = lambda t: t.reshape(bsz * seq, t.shape[-1])
    v_first = None
    for layer in range(w_in.shape[0]):
        g0 = _row(norm_g[layer, 0])
        (w_sz, w_sxbc, w_sdt, w_gq, w_gk, w_gv, w_gr, w_glo, w_lx, w_lg, w_rw, w_gates) = _split_cols(
            w_in[layer], IN_SIZES)
        y_ssd = _ssd_mixer(x, p, layer, g0, w_sz, w_sxbc, w_sdt)
        y_gla = _gla_mixer(x, p, layer, g0, w_gq, w_gk, w_gv, w_gr, w_glo)
        y_lru = _lru_mixer(x, p, layer, g0, w_lx, w_lg)
        y_rwkv, v_first = _rwkv_mixer(x, p, layer, g0, w_rw, v_first)
        x2d = _merge(flat(x), [flat(y_ssd), flat(y_gla), flat(y_lru), flat(y_rwkv)], p, layer, g0, w_gates)
        x = _ffn(x2d, p, layer).reshape(bsz, seq, D_MODEL)
    return x
```

```python
import functools

import jax
import jax.numpy as jnp
from jax import lax
from jax.experimental import pallas as pl
from jax.experimental.pallas import tpu as pltpu

F32 = jnp.float32
BF16 = jnp.bfloat16

D_MODEL = 1024
WIDTH = 512
RMS_EPS = 1e-6
CONV_WIDTH = 4
SSD_HEADS = 8
SSD_HEADDIM = 64
SSD_STATE = 64
SSD_CHUNK = 128
SSD_XBC = WIDTH + 4 * SSD_STATE
GLA_HEADS = 4
GLA_DK = 64
GLA_DV = 128
GLA_LOWRANK = 16
GLA_TAU = 16.0
GLA_CHUNK = 64
LRU_C = 8.0
RWKV_HEADS = 8
RWKV_HEAD = 64
RWKV_CHUNK = 64
RWKV_GN_EPS = 64e-5
RWKV_LORAS = (32, 32, 96)
RWKV_V_LORA = 32
FFN_HIDDEN = 2816
FFN_SPLIT = 2
LANE = 128
SUBLANE = 8
TIME_TILE = 512
TOKEN_TILE = 512
VMEM_LIMIT_BYTES = 56 * 1024 * 1024

IN_SIZES = (WIDTH, SSD_XBC, SSD_HEADS,
            GLA_HEADS * GLA_DK, GLA_HEADS * GLA_DK, WIDTH, WIDTH, GLA_LOWRANK,
            WIDTH, WIDTH,
            3 * WIDTH + sum(RWKV_LORAS),
            4 * D_MODEL)
N_IN = sum(IN_SIZES)


def _rmsn(x):
    return x * lax.rsqrt(jnp.mean(x * x, axis=-1, keepdims=True) + RMS_EPS)


def _dot(a, b):
    return jnp.dot(a.astype(BF16), b.astype(BF16), preferred_element_type=F32)


def _dot_nt(a, b):
    return lax.dot_general(a.astype(BF16), b.astype(BF16), (((1,), (1,)), ((), ())),
                           preferred_element_type=F32)


def _dot_tn(a, b):
    return lax.dot_general(a.astype(BF16), b.astype(BF16), (((0,), (0,)), ((), ())),
                           preferred_element_type=F32)


def _split3(x):
    x1 = x.astype(BF16)
    r1 = x - x1.astype(F32)
    x2 = r1.astype(BF16)
    x3 = (r1 - x2.astype(F32)).astype(BF16)
    return x1, x2, x3


def _sel_dot(sel, x):
    s = sel.astype(BF16)
    return sum(jnp.dot(s, p, preferred_element_type=F32) for p in _split3(x))


def _dot_sel(x, sel):
    s = sel.astype(BF16)
    return sum(jnp.dot(p, s, preferred_element_type=F32) for p in _split3(x))


def _iota(shape, axis):
    return lax.broadcasted_iota(jnp.int32, shape, axis)


def _tril(n, m=None, strict=False):
    m = n if m is None else m
    r, c = _iota((n, m), 0), _iota((n, m), 1)
    return (r > c) if strict else (r >= c)


def _shift_rows(x, tail, j):
    r = pltpu.roll(x, j, 0)
    pr = pltpu.roll(tail, j, 0)
    rid = _iota(tail.shape, 0)
    head = jnp.where(rid < j, pr, r[:SUBLANE])
    return jnp.concatenate([head, r[SUBLANE:]], axis=0)


def _causal_conv(x, tail, w, b):
    y = x * w[CONV_WIDTH - 1:CONV_WIDTH] + b
    for j in range(1, CONV_WIDTH):
        y = y + _shift_rows(x, tail, j) * w[CONV_WIDTH - 1 - j:CONV_WIDTH - j]
    return y


def _expand_heads(v, n_heads, width):
    rows = v.shape[0]
    return jnp.concatenate(
        [jnp.broadcast_to(v[:, h:h + 1], (rows, width)) for h in range(n_heads)], axis=1)


def _sigmoid(x):
    return jax.nn.sigmoid(x)


def _silu(x):
    return x * jax.nn.sigmoid(x)


def _normed_input(x_ref, g_ref):
    return (_rmsn(x_ref[0]) * g_ref[...]).astype(BF16)


def _ssd_kernel(x_ref, g_ref, wz_ref, wxbc_ref, wdt_ref, cw_ref, cb_ref, dtb_ref, alog_ref,
                dsk_ref, ng_ref, o_ref, tail_ref, st_ref, *, tt):
    c_len = SSD_CHUNK
    n_st = 2 * SSD_STATE

    @pl.when(pl.program_id(1) == 0)
    def _():
        tail_ref[...] = jnp.zeros_like(tail_ref)
        st_ref[...] = jnp.zeros_like(st_ref)

    h = _normed_input(x_ref, g_ref)
    z = jnp.dot(h, wz_ref[...], preferred_element_type=F32)
    xbc_raw = jnp.dot(h, wxbc_ref[...], preferred_element_type=F32)
    dt_raw = jnp.dot(h, wdt_ref[...], preferred_element_type=F32)
    xbc = _silu(_causal_conv(xbc_raw, tail_ref[...], cw_ref[...], cb_ref[...]))
    tail_ref[...] = xbc_raw[tt - SUBLANE:]
    dt = jax.nn.softplus(dt_raw + dtb_ref[...])
    da = dt * (-jnp.exp(alog_ref[...]))

    ltri = _tril(c_len).astype(F32)
    tril = _tril(c_len)
    lane = _iota((c_len, LANE), 1)
    low = lane < SSD_HEADDIM
    st_mask = (_iota((n_st, WIDTH), 0) < SSD_STATE) == (_iota((n_st, WIDTH), 1) < WIDTH // 2)
    st = st_ref[...]
    ys = []
    for c in range(tt // c_len):
        sl = slice(c * c_len, (c + 1) * c_len)
        xs = xbc[sl, :WIDTH]
        bm = xbc[sl, WIDTH:WIDTH + n_st]
        cm = xbc[sl, WIDTH + n_st:]
        dtc = dt[sl]
        cum = _sel_dot(ltri, da[sl])
        cum_t = cum.T
        dt_t = dtc.T
        cb = (_dot_nt(jnp.where(low, cm, 0.0), bm), _dot_nt(jnp.where(low, 0.0, cm), bm))
        pairs = []
        for hp in range(SSD_HEADS // 2):
            cbg = cb[hp // 2]
            ws = []
            for hd in (2 * hp, 2 * hp + 1):
                seg = cum[:, hd:hd + 1] - cum_t[hd:hd + 1, :]
                m = jnp.exp(jnp.where(tril, seg, -jnp.inf)) * dt_t[hd:hd + 1, :]
                ws.append((cbg * m).astype(BF16))
            xp = xs[:, hp * LANE:(hp + 1) * LANE]
            rhs = jnp.concatenate([jnp.where(low, xp, 0.0), jnp.where(low, 0.0, xp)], axis=0)
            pairs.append(_dot(jnp.concatenate(ws, axis=1), rhs))
        y = jnp.concatenate(pairs, axis=1)
        y = y + _dot(cm, st) * _expand_heads(jnp.exp(cum), SSD_HEADS, SSD_HEADDIM)
        ys.append(y + dsk_ref[...] * xs)
        cum_last = cum[c_len - 1:c_len]
        decay = jnp.exp(cum_last - cum) * dtc
        upd = _dot_tn(bm, xs * _expand_heads(decay, SSD_HEADS, SSD_HEADDIM))
        st = (st * _expand_heads(jnp.exp(cum_last), SSD_HEADS, SSD_HEADDIM)
              + jnp.where(st_mask, upd, 0.0))
    st_ref[...] = st
    y = jnp.concatenate(ys, axis=0) * _silu(z)
    half = WIDTH // 2
    y = jnp.concatenate([_rmsn(y[:, :half]), _rmsn(y[:, half:])], axis=1)
    o_ref[0] = (y * ng_ref[...]).astype(o_ref.dtype)


def _gla_kernel(x_ref, g_ref, wq_ref, wk_ref, wv_ref, wr_ref, wlo_ref, aw_ref, ab_ref, ng_ref,
                o_ref, st_ref, *, tt):
    c_len = GLA_CHUNK
    qk = GLA_HEADS * GLA_DK

    @pl.when(pl.program_id(1) == 0)
    def _():
        st_ref[...] = jnp.zeros_like(st_ref)

    h = _normed_input(x_ref, g_ref)
    q = jnp.dot(h, wq_ref[...], preferred_element_type=F32) * (GLA_DK ** -0.5)
    k = jnp.dot(h, wk_ref[...], preferred_element_type=F32)
    v = jnp.dot(h, wv_ref[...], preferred_element_type=F32)
    r = jnp.dot(h, wr_ref[...], preferred_element_type=F32)
    lo = jnp.dot(h, wlo_ref[...], preferred_element_type=F32)
    log_a = jax.nn.log_sigmoid(_dot(lo, aw_ref[...]) + ab_ref[...]) / GLA_TAU

    ltri = _tril(c_len).astype(F32)
    tril_stack = (_iota((GLA_HEADS * c_len, c_len), 0) & (c_len - 1)) >= _iota((GLA_HEADS * c_len, c_len), 1)
    head_of_lane = _iota((c_len, qk), 1) >> 6
    head_of_lane_dv = _iota((GLA_DV, qk), 1) >> 6
    st = st_ref[...]
    ys = []
    for c in range(tt // c_len):
        sl = slice(c * c_len, (c + 1) * c_len)
        b = _sel_dot(ltri, log_a[sl])
        ref = b[c_len // 2:c_len // 2 + 1]
        b_last = b[c_len - 1:c_len]
        qc, kc, vc = q[sl], k[sl], v[sl]
        qe = qc * jnp.exp(b - ref)
        ke = kc * jnp.exp(ref - b)
        qb = qc * jnp.exp(b)
        kl = kc * jnp.exp(b_last - b)
        stack = lambda a: jnp.concatenate(
            [jnp.where(head_of_lane == hd, a, 0.0) for hd in range(GLA_HEADS)], axis=0)
        att = jnp.where(tril_stack, _dot_nt(stack(qe), ke), 0.0)
        inter = _dot_nt(stack(qb), st)
        outs = []
        for hd in range(GLA_HEADS):
            rows = slice(hd * c_len, (hd + 1) * c_len)
            outs.append(_dot(att[rows], vc[:, hd * GLA_DV:(hd + 1) * GLA_DV]) + inter[rows])
        ys.append(jnp.concatenate(outs, axis=1))
        full = _dot_tn(vc, kl)
        upd = sum(jnp.where(head_of_lane_dv == hd, full[hd * GLA_DV:(hd + 1) * GLA_DV], 0.0)
                  for hd in range(GLA_HEADS))
        st = st * jnp.exp(b_last) + upd
    st_ref[...] = st
    y = jnp.concatenate(ys, axis=0)
    o = jnp.concatenate(
        [_rmsn(y[:, hd * GLA_DV:(hd + 1) * GLA_DV]) * ng_ref[...] for hd in range(GLA_HEADS)], axis=1)
    o_ref[0] = (o * _silu(r)).astype(o_ref.dtype)


def _lru_kernel(x_ref, g_ref, wx_ref, wg_ref, cw_ref, cb_ref, wa_ref, ba_ref, wi_ref, bi_ref,
                lam_ref, o_ref, tail_ref, carry_ref, *, tt):
    @pl.when(pl.program_id(1) == 0)
    def _():
        tail_ref[...] = jnp.zeros_like(tail_ref)
        carry_ref[...] = jnp.zeros_like(carry_ref)

    h = _normed_input(x_ref, g_ref)
    xr = jnp.dot(h, wx_ref[...], preferred_element_type=F32)
    gate = jnp.dot(h, wg_ref[...], preferred_element_type=F32)
    xb = _causal_conv(xr, tail_ref[...], cw_ref[...], cb_ref[...])
    tail_ref[...] = xr[tt - SUBLANE:]
    r = _sigmoid(_dot(xb, wa_ref[...]) + ba_ref[...])
    i = _sigmoid(_dot(xb, wi_ref[...]) + bi_ref[...])
    log_a = -LRU_C * r * jax.nn.softplus(-lam_ref[...])
    a = jnp.exp(log_a)
    u = jnp.sqrt(1.0 - jnp.exp(2.0 * log_a)) * (i * xb)

    rid = _iota((tt, WIDTH), 0) & (SUBLANE - 1)
    s = 1
    while s < SUBLANE:
        m = rid >= s
        u = jnp.where(m, a * pltpu.roll(u, s, 0) + u, u)
        a = jnp.where(m, a * pltpu.roll(a, s, 0), a)
        s *= 2
    carry = carry_ref[...]
    blocks = []
    for blk in range(tt // SUBLANE):
        sl = slice(blk * SUBLANE, (blk + 1) * SUBLANE)
        hb = u[sl] + a[sl] * carry
        blocks.append(hb)
        carry = jnp.broadcast_to(hb[SUBLANE - 1:], (SUBLANE, WIDTH))
    carry_ref[...] = carry
    hs = jnp.concatenate(blocks, axis=0)
    o_ref[0] = (hs * jax.nn.gelu(gate, approximate=True)).astype(o_ref.dtype)


def _rwkv_kernel(*refs, tt, has_vres):
    (x_ref, g_ref, wr_ref, wk_ref, wv_ref, wlo_ref, mur_ref, muk_ref, muv_ref, mulo_ref,
     w0_ref, ww2_ref, a0_ref, aw2_ref, gw2_ref, kk_ref, ka_ref, rk_ref, lng_ref, lnb_ref,
     hsum_ref) = refs[:21]
    if has_vres:
        v0_ref, vw2_ref, vf_ref, o_ref = refs[21:25]
        scratch = refs[25:]
    else:
        o_ref, vfo_ref = refs[21:23]
        scratch = refs[23:]
    (tr_ref, tk_ref, tv_ref, tlo_ref, s_ref, r_s, a_s, b_s, k_s, v_s, w_s, y_s) = scratch
    c_len = RWKV_CHUNK
    n = RWKV_HEAD

    @pl.when(pl.program_id(1) == 0)
    def _():
        for ref in (tr_ref, tk_ref, tv_ref, tlo_ref, s_ref):
            ref[...] = jnp.zeros_like(ref)

    h = _normed_input(x_ref, g_ref)

    def shifted(w_ref, tail_ref, mu_ref):
        p = jnp.dot(h, w_ref[...], preferred_element_type=F32)
        prev = _shift_rows(p, tail_ref[...], 1)
        tail_ref[...] = p[tt - SUBLANE:]
        return p + (prev - p) * mu_ref[...]

    r = shifted(wr_ref, tr_ref, mur_ref)
    k = shifted(wk_ref, tk_ref, muk_ref)
    v = shifted(wv_ref, tv_ref, muv_ref)
    lo = shifted(wlo_ref, tlo_ref, mulo_ref)
    w_lo, a_lo, g_lo = lo[:, :LANE], lo[:, LANE:2 * LANE], lo[:, 2 * LANE:3 * LANE]
    w_log = -jax.nn.softplus(-(w0_ref[...] + _dot(jnp.tanh(w_lo), ww2_ref[...]))) - 0.5
    log_w = -jnp.exp(w_log)
    a = _sigmoid(a0_ref[...] + _dot(a_lo, aw2_ref[...]))
    g = _dot(_sigmoid(g_lo), gw2_ref[...])
    if has_vres:
        v = v + (vf_ref[0] - v) * _sigmoid(v0_ref[...] + _dot(lo[:, 3 * LANE:], vw2_ref[...]))
    else:
        vfo_ref[0] = v
    hsum = hsum_ref[...]
    kk = k * kk_ref[...]
    kk = kk / jnp.maximum(jnp.sqrt(_dot_sel(kk * kk, hsum)), 1e-12)
    k = k * (1.0 + (a - 1.0) * ka_ref[...])
    bonus = _dot_sel(r * k * rk_ref[...], hsum) * v
    r_s[...] = r
    a_s[...] = -kk
    b_s[...] = kk * a
    k_s[...] = k
    v_s[...] = v
    w_s[...] = log_w

    ltri = _tril(c_len).astype(F32)
    tril2 = _iota((c_len, 2 * c_len), 0) >= (_iota((c_len, 2 * c_len), 1) & (c_len - 1))
    stril = _tril(c_len, strict=True)
    eye =(_iota((c_len, c_len), 0) == _iota((c_len, c_len), 1)).astype(F32)

    def chunk(c, carry):
        rows = pl.ds(pl.multiple_of(c * c_len, c_len), c_len)
        lw = w_s[rows, :]
        lc = _sel_dot(ltri, lw)
        e_pos = jnp.exp(lc)
        e_neg = jnp.exp(-lc)
        rt = r_s[rows, :] * e_pos
        at = a_s[rows, :] * jnp.exp(lc - lw)
        bt = b_s[rows, :] * e_neg
        kt = k_s[rows, :] * e_neg
        vv = v_s[rows, :]
        e_last = e_pos[c_len - 1:c_len]
        state = s_ref[...]
        heads = range(RWKV_HEADS)
        lane = lambda t, hd: t[:, hd * n:(hd + 1) * n]
        bk = [jnp.concatenate([lane(bt, hd), lane(kt, hd)], axis=0) for hd in heads]
        m = [_dot_nt(jnp.concatenate([lane(at, hd), lane(rt, hd)], axis=0), bk[hd]) for hd in heads]
        p = [jnp.where(stril, mh[:c_len, :c_len], 0.0) for mh in m]
        arbk = [jnp.where(tril2, mh[c_len:], 0.0) for mh in m]
        lakv = [_dot(jnp.where(stril, m[hd][:c_len, c_len:], 0.0), lane(vv, hd)) for hd in heads]
        inv = [eye + ph for ph in p]
        span = 2
        while span < c_len:
            p = [_dot(ph, ph) for ph in p]
            inv = [inv[hd] + _dot(inv[hd], p[hd]) for hd in heads]
            span *= 2
        tz = [_dot(inv[hd], jnp.concatenate([lane(at, hd), lakv[hd]], axis=1)) for hd in heads]
        gs = [_dot_nt(jnp.concatenate([tz[hd][:, :n], lane(rt, hd)], axis=0), lane(state, hd)) for hd in heads]
        uv = [jnp.concatenate([gs[hd][:c_len] + tz[hd][:, n:], lane(vv, hd)], axis=0) for hd in heads]
        y_parts = [gs[hd][c_len:] + _dot(arbk[hd], uv[hd]) for hd in heads]
        s_parts = [lane(state, hd) + _dot_tn(uv[hd], bk[hd]) for hd in heads]
        y_s[rows, :] = jnp.concatenate(y_parts, axis=1)
        s_ref[...] = jnp.concatenate(s_parts, axis=1) * e_last
        return carry

    lax.fori_loop(0, tt // c_len, chunk, 0)

    y = y_s[...]
    mean = _dot_sel(y, hsum) * (1.0 / n)
    d = y - mean
    var = _dot_sel(d * d, hsum) * (1.0 / n)
    y = d * lax.rsqrt(var + RWKV_GN_EPS) * lng_ref[...] + lnb_ref[...] + bonus
    o_ref[0] = (y * g).astype(o_ref.dtype)


def _merge_kernel(x_ref, y0_ref, y1_ref, y2_ref, y3_ref, g0_ref, g1_ref, wgate_ref, wbr_ref, wout_ref,
                  o_ref):
    x = x_ref[...]
    h = (_rmsn(x) * g0_ref[...]).astype(BF16)
    acc = jnp.zeros(x.shape, F32)
    for kb, y_ref in enumerate((y0_ref, y1_ref, y2_ref, y3_ref)):
        logits = jnp.dot(h, wgate_ref[:, kb * D_MODEL:(kb + 1) * D_MODEL], preferred_element_type=F32)
        branch = jnp.dot(y_ref[...], wbr_ref[kb], preferred_element_type=F32)
        acc = acc + _sigmoid(logits) * branch
    o = jnp.dot(acc.astype(BF16), wout_ref[...], preferred_element_type=F32)
    o_ref[...] = x + _rmsn(o) * g1_ref[...]


def _ffn_kernel(x_ref, g2_ref, g3_ref, wg_ref, wu_ref, wd_ref, o_ref):
    x = x_ref[...]
    h = (_rmsn(x) * g2_ref[...]).astype(BF16)
    f = jnp.zeros(x.shape, F32)
    step = FFN_HIDDEN // FFN_SPLIT
    for c in range(FFN_SPLIT):
        cols = slice(c * step, (c + 1) * step)
        gt = jnp.dot(h, wg_ref[:, cols], preferred_element_type=F32)
        up = jnp.dot(h, wu_ref[:, cols], preferred_element_type=F32)
        f = f + jnp.dot((_silu(gt) * up).astype(BF16), wd_ref[cols, :], preferred_element_type=F32)
    o_ref[...] = x + _rmsn(f) * g3_ref[...]


def _const_spec(a):
    nd = a.ndim
    return pl.BlockSpec(a.shape, lambda *_: (0,) * nd, pipeline_mode=pl.Buffered(1))


def _time_tile(seq):
    return min(TIME_TILE, seq)


def _mixer_call(body, name, x, consts, extra_in=(), n_out_f32=0, scratch=(), **static):
    bsz, seq, _ = x.shape
    tt = _time_tile(seq)
    assert seq % tt == 0 and tt % SSD_CHUNK == 0
    tile = lambda width: pl.BlockSpec((1, tt, width), lambda b, t: (b, t, 0))
    in_specs = [tile(D_MODEL)] + [_const_spec(c) for c in consts] + [tile(e.shape[-1]) for e in extra_in]
    out_shape = [jax.ShapeDtypeStruct((bsz, seq, WIDTH), BF16)]
    out_shape += [jax.ShapeDtypeStruct((bsz, seq, WIDTH), F32)] * n_out_f32
    out_specs = [tile(WIDTH)] * len(out_shape)
    return pl.pallas_call(
        functools.partial(body, tt=tt, **static),
        grid=(bsz, seq // tt),
        in_specs=in_specs,
        out_specs=out_specs,
        out_shape=out_shape,
        scratch_shapes=list(scratch),
        compiler_params=pltpu.CompilerParams(
            dimension_semantics=("parallel", "arbitrary"), vmem_limit_bytes=VMEM_LIMIT_BYTES),
        name=name,
    )(x, *consts, *extra_in)


def _token_call(body, name, x2d, tiles, consts):
    tokens = x2d.shape[0]
    tm = min(TOKEN_TILE, tokens)
    assert tokens % tm == 0
    tile = lambda width: pl.BlockSpec((tm, width), lambda i: (i, 0))
    return pl.pallas_call(
        body,
        grid=(tokens // tm,),
        in_specs=[tile(D_MODEL)] + [tile(t.shape[-1]) for t in tiles] + [_const_spec(c) for c in consts],
        out_specs=tile(D_MODEL),
        out_shape=jax.ShapeDtypeStruct(x2d.shape, F32),
        compiler_params=pltpu.CompilerParams(
            dimension_semantics=("parallel",), vmem_limit_bytes=VMEM_LIMIT_BYTES),
        name=name,
    )(x2d, *tiles, *consts)


def _row(v, width=None):
    v = v.reshape(1, -1).astype(F32)
    if width is not None and v.shape[1] < width:
        v = jnp.pad(v, ((0, 0), (0, width - v.shape[1])))
    return v


def _pad_to(w, rows=None, cols=None):
    rows = w.shape[0] if rows is None else rows
    cols = w.shape[1] if cols is None else cols
    return jnp.pad(w, ((0, rows - w.shape[0]), (0, cols - w.shape[1])))


def _split_cols(w, sizes):
    out, start = [], 0
    for nsz in sizes:
        out.append(w[:, start:start + nsz])
        start += nsz
    return out


def _block_diag(w):
    nb, bi, bj = w.shape
    eye = jnp.eye(nb, dtype=w.dtype)
    return jnp.einsum('hij,hg->higj', w, eye).reshape(nb * bi, nb * bj)


def _ssd_mixer(x, p, layer, g0, w_z, w_xbc, w_dt):
    vmem = lambda *shape: pltpu.VMEM(shape, F32)
    y, = _mixer_call(
        _ssd_kernel, f"ssd_{layer}", x,
        [g0, w_z.astype(BF16), w_xbc.astype(BF16), _pad_to(w_dt, cols=LANE).astype(BF16),
         p['ssd_conv_w'][layer], _row(p['ssd_conv_b'][layer]), _row(p['ssd_dt_bias'][layer], LANE),
         _row(p['ssd_a_log'][layer], LANE), _row(jnp.repeat(p['ssd_d'][layer], SSD_HEADDIM)),
         _row(p['ssd_norm_g'][layer])],
        scratch=[vmem(SUBLANE, SSD_XBC), vmem(2 * SSD_STATE, WIDTH)])
    return y


def _gla_mixer(x, p, layer, g0, w_q, w_k, w_v, w_r, w_lo):
    y, = _mixer_call(
        _gla_kernel, f"gla_{layer}", x,
        [g0, w_q.astype(BF16), w_k.astype(BF16), w_v.astype(BF16), w_r.astype(BF16),
         _pad_to(w_lo, cols=LANE).astype(BF16), _pad_to(p['gla_alpha_w'][layer], rows=LANE).astype(BF16),
         _row(p['gla_alpha_b'][layer]), _row(p['gla_norm_g'][layer])],
        scratch=[pltpu.VMEM((GLA_DV, GLA_HEADS * GLA_DK), F32)])
    return y


def _lru_mixer(x, p, layer, g0, w_x, w_gate):
    y, = _mixer_call(
        _lru_kernel, f"lru_{layer}", x,
        [g0, w_x.astype(BF16), w_gate.astype(BF16), p['lru_conv_w'][layer], _row(p['lru_conv_b'][layer]),
         _block_diag(p['lru_wa'][layer]).astype(BF16), _row(p['lru_ba'][layer]),
         _block_diag(p['lru_wx'][layer]).astype(BF16), _row(p['lru_bx'][layer]),
         _row(p['lru_lambda'][layer])],
        scratch=[pltpu.VMEM((SUBLANE, WIDTH), F32)] * 2)
    return y


def _rwkv_mixer(x, p, layer, g0, w_rw, v_first):
    bf = lambda w: w.astype(BF16)
    tt = _time_tile(x.shape[1])
    sizes = (WIDTH, WIDTH, WIDTH) + RWKV_LORAS
    w_r, w_k, w_v, w_wlo, w_alo, w_glo = _split_cols(w_rw, sizes)
    mu_r, mu_k, mu_v, mu_wlo, mu_alo, mu_glo = _split_cols(p['rwkv_mu'][layer].reshape(1, -1), sizes)
    has_vres = v_first is not None
    w_vlo = p['w_vres_in'][layer - 1] if has_vres else jnp.zeros((D_MODEL, RWKV_V_LORA), F32)
    lo_w = jnp.concatenate([_pad_to(w, cols=LANE) for w in (w_wlo, w_alo, w_glo, w_vlo)], axis=1)
    lo_mu = jnp.concatenate([_pad_to(m, cols=LANE) for m in (mu_wlo, mu_alo, mu_glo)]
                            + [jnp.zeros((1, LANE), F32)], axis=1)
    head_sum = (jnp.arange(WIDTH)[:, None] // RWKV_HEAD == jnp.arange(WIDTH)[None, :] // RWKV_HEAD)
    consts = [g0, bf(w_r), bf(w_k), bf(w_v), bf(lo_w), mu_r, mu_k, mu_v, lo_mu,
              _row(p['rwkv_w0'][layer]), bf(_pad_to(p['rwkv_w_w2'][layer], rows=LANE)),
              _row(p['rwkv_a0'][layer]), bf(_pad_to(p['rwkv_a_w2'][layer], rows=LANE)),
              bf(_pad_to(p['rwkv_g_w2'][layer], rows=LANE)),
              _row(p['rwkv_k_k'][layer]), _row(p['rwkv_k_a'][layer]), _row(p['rwkv_r_k'][layer]),
              _row(p['rwkv_ln_g'][layer]), _row(p['rwkv_ln_b'][layer]), bf(head_sum)]
    vmem = lambda *shape: pltpu.VMEM(shape, F32)
    scratch = [vmem(SUBLANE, WIDTH)] * 4 + [vmem(RWKV_HEAD, WIDTH)] + [vmem(tt, WIDTH)] * 7
    if has_vres:
        consts += [_row(p['rwkv_v0'][layer - 1]), bf(_pad_to(p['rwkv_v_w2'][layer - 1], rows=LANE))]
        y, = _mixer_call(_rwkv_kernel, f"rwkv_{layer}", x, consts, extra_in=[v_first],
                         scratch=scratch, has_vres=True)
        return y, v_first
    return tuple(_mixer_call(_rwkv_kernel, f"rwkv_{layer}", x, consts, n_out_f32=1,
                             scratch=scratch, has_vres=False))


def _merge(x2d, ys, p, layer, g0, w_gates):
    return _token_call(
        _merge_kernel, f"merge_{layer}", x2d, ys,
        [g0, _row(p['norm_g'][layer, 1]), w_gates.astype(BF16), p['w_branch'][layer].astype(BF16),
         p['w_out'][layer].astype(BF16)])


def _ffn(x2d, p, layer):
    return _token_call(
        _ffn_kernel, f"ffn_{layer}", x2d, [],
        [_row(p['norm_g'][layer, 2]), _row(p['norm_g'][layer, 3]), p['ffn_w_gate'][layer].astype(BF16),
         p['ffn_w_up'][layer].astype(BF16), p['ffn_w_down'][layer].astype(BF16)])


def kernel(x, norm_g, w_in, w_vres_in, ssd_conv_w, ssd_conv_b, ssd_dt_bias, ssd_a_log, ssd_d, ssd_norm_g,
           gla_alpha_w, gla_alpha_b, gla_norm_g, lru_conv_w, lru_conv_b, lru_wa, lru_ba, lru_wx, lru_bx,
           lru_lambda, rwkv_mu, rwkv_w0, rwkv_w_w2, rwkv_a0, rwkv_a_w2, rwkv_v0, rwkv_v_w2, rwkv_g_w2,
           rwkv_k_k, rwkv_k_a, rwkv_r_k, rwkv_ln_g, rwkv_ln_b, w_branch, w_out,
           ffn_w_gate, ffn_w_up, ffn_w_down):
    p = dict(locals())
    bsz, seq, _ = x.shape
    flat = lambda t: t.reshape(bsz * seq, t.shape[-1])
    v_first = None
    for layer in range(w_in.shape[0]):
        g0 = _row(norm_g[layer, 0])
        (w_sz, w_sxbc, w_sdt, w_gq, w_gk, w_gv, w_gr, w_glo, w_lx, w_lg, w_rw, w_gates) = _split_cols(
            w_in[layer], IN_SIZES)
        y_ssd = _ssd_mixer(x, p, layer, g0, w_sz, w_sxbc, w_sdt)
        y_gla = _gla_mixer(x, p, layer, g0, w_gq, w_gk, w_gv, w_gr, w_glo)
        y_lru = _lru_mixer(x, p, layer, g0, w_lx, w_lg)
        y_rwkv, v_first = _rwkv_mixer(x, p, layer, g0, w_rw, v_first)
        x2d = _merge(flat(x), [flat(y_ssd), flat(y_gla), flat(y_lru), flat(y_rwkv)], p, layer, g0, w_gates)
        x = _ffn(x2d, p, layer).reshape(bsz, seq, D_MODEL)
    return x
```

```python
import functools

import jax
import jax.numpy as jnp
from jax import lax
from jax.experimental import pallas as pl
from jax.experimental.pallas import tpu as pltpu

F32 = jnp.float32
BF16 = jnp.bfloat16

D_MODEL = 1024
WIDTH = 512
RMS_EPS = 1e-6
CONV_WIDTH = 4
SSD_HEADS = 8
SSD_HEADDIM = 64
SSD_STATE = 64
SSD_CHUNK = 128
SSD_XBC = WIDTH + 4 * SSD_STATE
GLA_HEADS = 4
GLA_DK = 64
GLA_DV = 128
GLA_LOWRANK = 16
GLA_TAU = 16.0
GLA_CHUNK = 64
LRU_C = 8.0
RWKV_HEADS = 8
RWKV_HEAD = 64
RWKV_CHUNK = 64
RWKV_GN_EPS = 64e-5
RWKV_LORAS = (32, 32, 96)
RWKV_V_LORA = 32
FFN_HIDDEN = 2816
FFN_SPLIT = 2
LANE = 128
SUBLANE = 8
TIME_TILE = 512
TOKEN_TILE = 512
VMEM_LIMIT_BYTES = 56 * 1024 * 1024

IN_SIZES = (WIDTH, SSD_XBC, SSD_HEADS,
            GLA_HEADS * GLA_DK, GLA_HEADS * GLA_DK, WIDTH, WIDTH, GLA_LOWRANK,
            WIDTH, WIDTH,
            3 * WIDTH + sum(RWKV_LORAS),
            4 * D_MODEL)
N_IN = sum(IN_SIZES)


def _rmsn(x):
    return x * lax.rsqrt(jnp.mean(x * x, axis=-1, keepdims=True) + RMS_EPS)


def _dot(a, b):
    return jnp.dot(a.astype(BF16), b.astype(BF16), preferred_element_type=F32)


def _dot_nt(a, b):
    return lax.dot_general(a.astype(BF16), b.astype(BF16), (((1,), (1,)), ((), ())),
                           preferred_element_type=F32)


def _dot_tn(a, b):
    return lax.dot_general(a.astype(BF16), b.astype(BF16), (((0,), (0,)), ((), ())),
                           preferred_element_type=F32)


def _split(x, parts):
    out = []
    for _ in range(parts - 1):
        hi = x.astype(BF16)
        out.append(hi)
        x = x - hi.astype(F32)
    return out + [x.astype(BF16)]


def _sel_dot(sel, x, parts=3):
    s = sel.astype(BF16)
    return sum(jnp.dot(s, p, preferred_element_type=F32) for p in _split(x, parts))


def _dot_sel(x, sel, parts=3):
    s = sel.astype(BF16)
    return sum(jnp.dot(p, s, preferred_element_type=F32) for p in _split(x, parts))


def _iota(shape, axis):
    return lax.broadcasted_iota(jnp.int32, shape, axis)


def _tril(n, m=None, strict=False):
    m = n if m is None else m
    r, c = _iota((n, m), 0), _iota((n, m), 1)
    return (r > c) if strict else (r >= c)


def _shift_rows(x, tail, j):
    tt, width = x.shape
    r3 = pltpu.roll(x.reshape(tt // SUBLANE, SUBLANE, width), j, 1)
    prev = jnp.concatenate([pltpu.roll(tail, j, 0)[None], r3[:-1]], axis=0)
    return jnp.where(_iota(r3.shape, 1) < j, prev, r3).reshape(tt, width)


def _causal_conv(x, tail, w, b):
    y = x * w[CONV_WIDTH - 1:CONV_WIDTH] + b
    for j in range(1, CONV_WIDTH):
        y = y + _shift_rows(x, tail, j) * w[CONV_WIDTH - 1 - j:CONV_WIDTH - j]
    return y


def _expand_heads(v, n_heads, width):
    rows = v.shape[0]
    return jnp.concatenate(
        [jnp.broadcast_to(v[:, h:h + 1], (rows, width)) for h in range(n_heads)], axis=1)


def _sigmoid(x):
    return jax.nn.sigmoid(x)


def _silu(x):
    return x * jax.nn.sigmoid(x)


def _normed_input(x_ref, g_ref):
    return (_rmsn(x_ref[0]) * g_ref[...]).astype(BF16)


def _ssd_kernel(x_ref, g_ref, wz_ref, wxbc_ref, wdt_ref, cw_ref, cb_ref, dtb_ref, alog_ref,
                dsk_ref, ng_ref, o_ref, tail_ref, st_ref, *, tt):
    c_len = SSD_CHUNK
    n_st = 2 * SSD_STATE

    @pl.when(pl.program_id(1) == 0)
    def _():
        tail_ref[...] = jnp.zeros_like(tail_ref)
        st_ref[...] = jnp.zeros_like(st_ref)

    h = _normed_input(x_ref, g_ref)
    z = jnp.dot(h, wz_ref[...], preferred_element_type=F32)
    xbc_raw = jnp.dot(h, wxbc_ref[...], preferred_element_type=F32)
    dt_raw = jnp.dot(h, wdt_ref[...], preferred_element_type=F32)
    xbc = _silu(_causal_conv(xbc_raw, tail_ref[...], cw_ref[...], cb_ref[...]))
    tail_ref[...] = xbc_raw[tt - SUBLANE:]
    dt = jax.nn.softplus(dt_raw + dtb_ref[...])
    da = dt * (-jnp.exp(alog_ref[...]))

    ltri = _tril(c_len).astype(F32)
    tril = _tril(c_len)
    lane = _iota((c_len, LANE), 1)
    low = lane < SSD_HEADDIM
    st_mask = (_iota((n_st, WIDTH), 0) < SSD_STATE) == (_iota((n_st, WIDTH), 1) < WIDTH // 2)
    chunks = range(tt // c_len)
    rows = lambda t, c: t[c * c_len:(c + 1) * c_len]
    expand = lambda t: _expand_heads(t, SSD_HEADS, SSD_HEADDIM)
    xs = [rows(xbc, c)[:, :WIDTH] for c in chunks]
    bm = [rows(xbc, c)[:, WIDTH:WIDTH + n_st] for c in chunks]
    cm = [rows(xbc, c)[:, WIDTH + n_st:] for c in chunks]
    cum = [_sel_dot(ltri, rows(da, c)) for c in chunks]
    cum_t = [t.T for t in cum]
    dt_t = [rows(dt, c).T for c in chunks]
    cb = [(_dot_nt(jnp.where(low, cm[c], 0.0), bm[c]), _dot_nt(jnp.where(low, 0.0, cm[c]), bm[c]))
          for c in chunks]
    upd = [_dot_tn(bm[c], xs[c] * expand(jnp.exp(cum[c][c_len - 1:c_len] - cum[c]) * rows(dt, c)))
           for c in chunks]
    intra = []
    for c in chunks:
        pairs = []
        for hp in range(SSD_HEADS // 2):
            ws = []
            for hd in (2 * hp, 2 * hp + 1):
                seg = cum[c][:, hd:hd + 1] - cum_t[c][hd:hd + 1, :]
                m = jnp.exp(jnp.where(tril, seg, -jnp.inf)) * dt_t[c][hd:hd + 1, :]
                ws.append((cb[c][hp // 2] * m).astype(BF16))
            xp = xs[c][:, hp * LANE:(hp + 1) * LANE]
            rhs = jnp.concatenate([jnp.where(low, xp, 0.0), jnp.where(low, 0.0, xp)], axis=0)
            pairs.append(_dot(jnp.concatenate(ws, axis=1), rhs))
        intra.append(jnp.concatenate(pairs, axis=1))
    sts = [st_ref[...]]
    for c in chunks:
        sts.append(sts[c] * expand(jnp.exp(cum[c][c_len - 1:c_len])) + jnp.where(st_mask, upd[c], 0.0))
    st_ref[...] = sts[-1]
    ys = [intra[c] + _dot(cm[c], sts[c]) * expand(jnp.exp(cum[c])) + dsk_ref[...] * xs[c] for c in chunks]
    y = jnp.concatenate(ys, axis=0) * _silu(z)
    half = WIDTH // 2
    y = jnp.concatenate([_rmsn(y[:, :half]), _rmsn(y[:, half:])], axis=1)
    o_ref[0] = (y * ng_ref[...]).astype(o_ref.dtype)


def _gla_kernel(x_ref, g_ref, wq_ref, wk_ref, wv_ref, wr_ref, wlo_ref, aw_ref, ab_ref, ng_ref,
                o_ref, st_ref, *, tt):
    c_len = GLA_CHUNK
    qk = GLA_HEADS * GLA_DK

    @pl.when(pl.program_id(1) == 0)
    def _():
        st_ref[...] = jnp.zeros_like(st_ref)

    h = _normed_input(x_ref, g_ref)
    q = jnp.dot(h, wq_ref[...], preferred_element_type=F32) * (GLA_DK ** -0.5)
    k = jnp.dot(h, wk_ref[...], preferred_element_type=F32)
    v = jnp.dot(h, wv_ref[...], preferred_element_type=F32)
    r = jnp.dot(h, wr_ref[...], preferred_element_type=F32)
    lo = jnp.dot(h, wlo_ref[...], preferred_element_type=F32)
    log_a = jax.nn.log_sigmoid(_dot(lo, aw_ref[...]) + ab_ref[...]) / GLA_TAU

    ltri = _tril(c_len).astype(F32)
    tril_stack = (_iota((GLA_HEADS * c_len, c_len), 0) & (c_len - 1)) >= _iota((GLA_HEADS * c_len, c_len), 1)
    head_of_lane = _iota((c_len, qk), 1) >> 6
    head_of_lane_dv = _iota((GLA_DV, qk), 1) >> 6
    chunks = range(tt // c_len)
    heads = range(GLA_HEADS)
    rows = lambda t, c: t[c * c_len:(c + 1) * c_len]
    stack = lambda t: jnp.concatenate([jnp.where(head_of_lane == hd, t, 0.0) for hd in heads], axis=0)
    b = [_sel_dot(ltri, rows(log_a, c)) for c in chunks]
    ref = [t[c_len // 2:c_len // 2 + 1] for t in b]
    b_last = [t[c_len - 1:c_len] for t in b]
    att = [jnp.where(tril_stack,
                     _dot_nt(stack(rows(q, c) * jnp.exp(b[c] - ref[c])), rows(k, c) * jnp.exp(ref[c] - b[c])),
                     0.0) for c in chunks]
    full = [_dot_tn(rows(v, c), rows(k, c) * jnp.exp(b_last[c] - b[c])) for c in chunks]
    sts = [st_ref[...]]
    for c in chunks:
        upd = sum(jnp.where(head_of_lane_dv == hd, full[c][hd * GLA_DV:(hd + 1) * GLA_DV], 0.0)
                  for hd in heads)
        sts.append(sts[c] * jnp.exp(b_last[c]) + upd)
    st_ref[...] = sts[-1]
    inter = [_dot_nt(stack(rows(q, c) * jnp.exp(b[c])), sts[c]) for c in chunks]
    ys = [jnp.concatenate(
        [_dot(att[c][hd * c_len:(hd + 1) * c_len], rows(v, c)[:, hd * GLA_DV:(hd + 1) * GLA_DV])
         + inter[c][hd * c_len:(hd + 1) * c_len] for hd in heads], axis=1) for c in chunks]
    y = jnp.concatenate(ys, axis=0)
    o = jnp.concatenate(
        [_rmsn(y[:, hd * GLA_DV:(hd + 1) * GLA_DV]) * ng_ref[...] for hd in range(GLA_HEADS)], axis=1)
    o_ref[0] = (o * _silu(r)).astype(o_ref.dtype)


def _lru_kernel(x_ref, g_ref, wx_ref, wg_ref, cw_ref, cb_ref, wa_ref, ba_ref, wi_ref, bi_ref,
                lam_ref, o_ref, tail_ref, carry_ref, *, tt):
    @pl.when(pl.program_id(1) == 0)
    def _():
        tail_ref[...] = jnp.zeros_like(tail_ref)
        carry_ref[...] = jnp.zeros_like(carry_ref)

    h = _normed_input(x_ref, g_ref)
    xr = jnp.dot(h, wx_ref[...], preferred_element_type=F32)
    gate = jnp.dot(h, wg_ref[...], preferred_element_type=F32)
    xb = _causal_conv(xr, tail_ref[...], cw_ref[...], cb_ref[...])
    tail_ref[...] = xr[tt - SUBLANE:]
    r = _sigmoid(_dot(xb, wa_ref[...]) + ba_ref[...])
    i = _sigmoid(_dot(xb, wi_ref[...]) + bi_ref[...])
    log_a = -LRU_C * r * jax.nn.softplus(-lam_ref[...])
    a = jnp.exp(log_a)
    u = jnp.sqrt(1.0 - jnp.exp(2.0 * log_a)) * (i * xb)

    n_blk = tt // SUBLANE
    a = a.reshape(n_blk, SUBLANE, WIDTH)
    u = u.reshape(n_blk, SUBLANE, WIDTH)
    rid = _iota(a.shape, 1)
    s = 1
    while s < SUBLANE:
        m = rid >= s
        u = jnp.where(m, a * pltpu.roll(u, s, 1) + u, u)
        a = jnp.where(m, a * pltpu.roll(a, s, 1), a)
        s *= 2
    carry = carry_ref[...]
    blocks = []
    for blk in range(n_blk):
        hb = u[blk] + a[blk] * carry
        blocks.append(hb)
        carry = jnp.broadcast_to(hb[SUBLANE - 1:], (SUBLANE, WIDTH))
    carry_ref[...] = carry
    hs = jnp.concatenate(blocks, axis=0)
    o_ref[0] = (hs * jax.nn.gelu(gate, approximate=True)).astype(o_ref.dtype)


def _rwkv_kernel(*refs, tt, has_vres):
    (x_ref, g_ref, wr_ref, wk_ref, wv_ref, wlo_ref, mur_ref, muk_ref, muv_ref, mulo_ref,
     w0_ref, ww2_ref, a0_ref, aw2_ref, gw2_ref, kk_ref, ka_ref, rk_ref, lng_ref, lnb_ref,
     hsum_ref) = refs[:21]
    if has_vres:
        v0_ref, vw2_ref, vf_ref, o_ref = refs[21:25]
        scratch = refs[25:]
    else:
        o_ref, vfo_ref = refs[21:23]
        scratch = refs[23:]
    (tr_ref, tk_ref, tv_ref, tlo_ref, s_ref) = scratch
    c_len = RWKV_CHUNK
    n = RWKV_HEAD

    @pl.when(pl.program_id(1) == 0)
    def _():
        for ref in (tr_ref, tk_ref, tv_ref, tlo_ref, s_ref):
            ref[...] = jnp.zeros_like(ref)

    h = _normed_input(x_ref, g_ref)

    def shifted(w_ref, tail_ref, mu_ref):
        p = jnp.dot(h, w_ref[...], preferred_element_type=F32)
        prev = _shift_rows(p, tail_ref[...], 1)
        tail_ref[...] = p[tt - SUBLANE:]
        return p + (prev - p) * mu_ref[...]

    r = shifted(wr_ref, tr_ref, mur_ref)
    k = shifted(wk_ref, tk_ref, muk_ref)
    v = shifted(wv_ref, tv_ref, muv_ref)
    lo = shifted(wlo_ref, tlo_ref, mulo_ref)
    w_lo, a_lo, g_lo = lo[:, :LANE], lo[:, LANE:2 * LANE], lo[:, 2 * LANE:3 * LANE]
    w_log = -jax.nn.softplus(-(w0_ref[...] + _dot(jnp.tanh(w_lo), ww2_ref[...]))) - 0.5
    log_w = -jnp.exp(w_log)
    a = _sigmoid(a0_ref[...] + _dot(a_lo, aw2_ref[...]))
    g = _dot(_sigmoid(g_lo), gw2_ref[...])
    if has_vres:
        v = v + (vf_ref[0] - v) * _sigmoid(v0_ref[...] + _dot(lo[:, 3 * LANE:], vw2_ref[...]))
    else:
        vfo_ref[0] = v
    hsum = hsum_ref[...]
    head_sum = lambda t: _expand_heads(_dot_sel(t, hsum, parts=2), RWKV_HEADS, n)
    kk = k * kk_ref[...]
    kk = kk / jnp.maximum(jnp.sqrt(head_sum(kk * kk)), 1e-12)
    k = k * (1.0 + (a - 1.0) * ka_ref[...])
    bonus = head_sum(r * k * rk_ref[...]) * v

    n_chunks = tt // c_len
    n_pairs = WIDTH // LANE
    ltri = _tril(c_len).astype(F32)
    lc = jnp.concatenate(
        [_sel_dot(ltri, log_w[c * c_len:(c + 1) * c_len]) for c in range(n_chunks)], axis=0)
    e_pos = jnp.exp(lc)
    e_neg = jnp.exp(-lc)
    rt = r * e_pos
    at = -kk * jnp.exp(lc - log_w)
    bt = kk * a * e_neg
    kt = k * e_neg

    row = _iota((c_len, LANE), 0)
    col = _iota((c_len, LANE), 1) & (n - 1)
    stril = row > col
    eye = (row == col).astype(F32)
    tril2 = _iota((c_len, 2 * LANE), 0) >= (_iota((c_len, 2 * LANE), 1) & (n - 1))
    low = _iota((c_len, LANE), 1) < n

    def bd(t):
        return jnp.concatenate([jnp.where(low, t, 0.0), jnp.where(low, 0.0, t)], axis=0)

    blocks = [(c, q) for c in range(n_chunks) for q in range(n_pairs)]
    blk = lambda t, cq: t[cq[0] * c_len:(cq[0] + 1) * c_len, cq[1] * LANE:(cq[1] + 1) * LANE]

    e_last_t = jnp.concatenate(
        [e_pos[(c + 1) * c_len - 1:(c + 1) * c_len] for c in range(n_chunks)], axis=0).T

    def decay_cols(cq):
        c, q = cq
        col_of = lambda hd: jnp.broadcast_to(e_last_t[hd * n:(hd + 1) * n, c:c + 1], (c_len, LANE))
        return jnp.where(low, col_of(2 * q), col_of(2 * q + 1))

    vbd = [bd(blk(v, cq)) for cq in blocks]
    m = [_dot_nt(jnp.concatenate([blk(at, cq), blk(rt, cq), eye], axis=0),
                 jnp.concatenate([bd(blk(bt, cq)), bd(blk(kt, cq))], axis=0)) for cq in blocks]
    p = [jnp.where(stril, mb[:c_len, :LANE], 0.0) for mb in m]
    rbk = [jnp.concatenate([jnp.where(tril2, mb[c_len:2 * c_len], 0.0), mb[2 * c_len:]], axis=0) for mb in m]
    lakv = [_dot(jnp.where(stril, mb[:c_len, LANE:], 0.0), vb) for mb, vb in zip(m, vbd)]
    inv = [eye + pb for pb in p]
    span = 2
    while span < c_len:
        p = [_dot(pb, bd(pb)) for pb in p]
        inv = [ib + _dot(ib, bd(pb)) for ib, pb in zip(inv, p)]
        span *= 2
    tz = [_dot(ib, jnp.concatenate([bd(blk(at, cq)), bd(lb)], axis=1))
          for ib, lb, cq in zip(inv, lakv, blocks)]
    ar = [jnp.concatenate([tzb[:, :LANE], blk(rt, cq)], axis=0) for tzb, cq in zip(tz, blocks)]
    e_cols = [decay_cols(cq) for cq in blocks]

    st = [s_ref[:, q * LANE:(q + 1) * LANE] for q in range(n_pairs)]
    y_rows = []
    for c in range(n_chunks):
        ids = range(c * n_pairs, (c + 1) * n_pairs)
        gs = [_dot(ar[i], bd(st[q])) for q, i in enumerate(ids)]
        uv = [jnp.concatenate([bd(gs[q][:c_len] + tz[i][:, LANE:]), vbd[i]], axis=0) for q, i in enumerate(ids)]
        yd = [_dot(rbk[i], uv[q]) for q, i in enumerate(ids)]
        y_rows.append(jnp.concatenate([gs[q][c_len:] + yd[q][:c_len] for q in range(n_pairs)], axis=1))
        st = [(st[q] + yd[q][c_len:]) * e_cols[i] for q, i in enumerate(ids)]
    s_ref[...] = jnp.concatenate(st, axis=1)

    y = jnp.concatenate(y_rows, axis=0)
    mean = head_sum(y) * (1.0 / n)
    d = y - mean
    var = head_sum(d * d) * (1.0 / n)
    y = d * lax.rsqrt(var + RWKV_GN_EPS) * lng_ref[...] + lnb_ref[...] + bonus
    o_ref[0] = (y * g).astype(o_ref.dtype)


def _merge_kernel(x_ref, y0_ref, y1_ref, y2_ref, y3_ref, g0_ref, g1_ref, wgate_ref, wbr_ref, wout_ref,
                  o_ref):
    x = x_ref[...]
    h = (_rmsn(x) * g0_ref[...]).astype(BF16)
    acc = jnp.zeros(x.shape, F32)
    for kb, y_ref in enumerate((y0_ref, y1_ref, y2_ref, y3_ref)):
        logits = jnp.dot(h, wgate_ref[:, kb * D_MODEL:(kb + 1) * D_MODEL], preferred_element_type=F32)
        branch = jnp.dot(y_ref[...], wbr_ref[kb], preferred_element_type=F32)
        acc = acc + _sigmoid(logits) * branch
    o = jnp.dot(acc.astype(BF16), wout_ref[...], preferred_element_type=F32)
    o_ref[...] = x + _rmsn(o) * g1_ref[...]


def _ffn_kernel(x_ref, g2_ref, g3_ref, wg_ref, wu_ref, wd_ref, o_ref):
    x = x_ref[...]
    h = (_rmsn(x) * g2_ref[...]).astype(BF16)
    f = jnp.zeros(x.shape, F32)
    step = FFN_HIDDEN // FFN_SPLIT
    for c in range(FFN_SPLIT):
        cols = slice(c * step, (c + 1) * step)
        gt = jnp.dot(h, wg_ref[:, cols], preferred_element_type=F32)
        up = jnp.dot(h, wu_ref[:, cols], preferred_element_type=F32)
        f = f + jnp.dot((_silu(gt) * up).astype(BF16), wd_ref[cols, :], preferred_element_type=F32)
    o_ref[...] = x + _rmsn(f) * g3_ref[...]


def _const_spec(a):
    nd = a.ndim
    return pl.BlockSpec(a.shape, lambda *_: (0,) * nd, pipeline_mode=pl.Buffered(1))


def _time_tile(seq):
    return min(TIME_TILE, seq)


def _mixer_call(body, name, x, consts, extra_in=(), n_out_f32=0, scratch=(), **static):
    bsz, seq, _ = x.shape
    tt = _time_tile(seq)
    assert seq % tt == 0 and tt % SSD_CHUNK == 0
    tile = lambda width: pl.BlockSpec((1, tt, width), lambda b, t: (b, t, 0))
    in_specs = [tile(D_MODEL)] + [_const_spec(c) for c in consts] + [tile(e.shape[-1]) for e in extra_in]
    out_shape = [jax.ShapeDtypeStruct((bsz, seq, WIDTH), BF16)]
    out_shape += [jax.ShapeDtypeStruct((bsz, seq, WIDTH), F32)] * n_out_f32
    out_specs = [tile(WIDTH)] * len(out_shape)
    return pl.pallas_call(
        functools.partial(body, tt=tt, **static),
        grid=(bsz, seq // tt),
        in_specs=in_specs,
        out_specs=out_specs,
        out_shape=out_shape,
        scratch_shapes=list(scratch),
        compiler_params=pltpu.CompilerParams(
            dimension_semantics=("parallel", "arbitrary"), vmem_limit_bytes=VMEM_LIMIT_BYTES),
        name=name,
    )(x, *consts, *extra_in)


def _token_call(body, name, x2d, tiles, consts):
    tokens = x2d.shape[0]
    tm = min(TOKEN_TILE, tokens)
    assert tokens % tm == 0
    tile = lambda width: pl.BlockSpec((tm, width), lambda i: (i, 0))
    return pl.pallas_call(
        body,
        grid=(tokens // tm,),
        in_specs=[tile(D_MODEL)] + [tile(t.shape[-1]) for t in tiles] + [_const_spec(c) for c in consts],
        out_specs=tile(D_MODEL),
        out_shape=jax.ShapeDtypeStruct(x2d.shape, F32),
        compiler_params=pltpu.CompilerParams(
            dimension_semantics=("parallel",), vmem_limit_bytes=VMEM_LIMIT_BYTES),
        name=name,
    )(x2d, *tiles, *consts)


def _row(v, width=None):
    v = v.reshape(1, -1).astype(F32)
    if width is not None and v.shape[1] < width:
        v = jnp.pad(v, ((0, 0), (0, width - v.shape[1])))
    return v


def _pad_to(w, rows=None, cols=None):
    rows = w.shape[0] if rows is None else rows
    cols = w.shape[1] if cols is None else cols
    return jnp.pad(w, ((0, rows - w.shape[0]), (0, cols - w.shape[1])))


def _split_cols(w, sizes):
    out, start = [], 0
    for nsz in sizes:
        out.append(w[:, start:start + nsz])
        start += nsz
    return out


def _block_diag(w):
    nb, bi, bj = w.shape
    eye = jnp.eye(nb, dtype=w.dtype)
    return jnp.einsum('hij,hg->higj', w, eye).reshape(nb * bi, nb * bj)


def _ssd_mixer(x, p, layer, g0, w_z, w_xbc, w_dt):
    vmem = lambda *shape: pltpu.VMEM(shape, F32)
    y, = _mixer_call(
        _ssd_kernel, f"ssd_{layer}", x,
        [g0, w_z.astype(BF16), w_xbc.astype(BF16), _pad_to(w_dt, cols=LANE).astype(BF16),
         p['ssd_conv_w'][layer], _row(p['ssd_conv_b'][layer]), _row(p['ssd_dt_bias'][layer], LANE),
         _row(p['ssd_a_log'][layer], LANE), _row(jnp.repeat(p['ssd_d'][layer], SSD_HEADDIM)),
         _row(p['ssd_norm_g'][layer])],
        scratch=[vmem(SUBLANE, SSD_XBC), vmem(2 * SSD_STATE, WIDTH)])
    return y


def _gla_mixer(x, p, layer, g0, w_q, w_k, w_v, w_r, w_lo):
    y, = _mixer_call(
        _gla_kernel, f"gla_{layer}", x,
        [g0, w_q.astype(BF16), w_k.astype(BF16), w_v.astype(BF16), w_r.astype(BF16),
         _pad_to(w_lo, cols=LANE).astype(BF16), _pad_to(p['gla_alpha_w'][layer], rows=LANE).astype(BF16),
         _row(p['gla_alpha_b'][layer]), _row(p['gla_norm_g'][layer])],
        scratch=[pltpu.VMEM((GLA_DV, GLA_HEADS * GLA_DK), F32)])
    return y


def _lru_mixer(x, p, layer, g0, w_x, w_gate):
    y, = _mixer_call(
        _lru_kernel, f"lru_{layer}", x,
        [g0, w_x.astype(BF16), w_gate.astype(BF16), p['lru_conv_w'][layer], _row(p['lru_conv_b'][layer]),
         _block_diag(p['lru_wa'][layer]).astype(BF16), _row(p['lru_ba'][layer]),
         _block_diag(p['lru_wx'][layer]).astype(BF16), _row(p['lru_bx'][layer]),
         _row(p['lru_lambda'][layer])],
        scratch=[pltpu.VMEM((SUBLANE, WIDTH), F32)] * 2)
    return y


def _rwkv_mixer(x, p, layer, g0, w_rw, v_first):
    bf = lambda w: w.astype(BF16)
    tt = _time_tile(x.shape[1])
    sizes = (WIDTH, WIDTH, WIDTH) + RWKV_LORAS
    w_r, w_k, w_v, w_wlo, w_alo, w_glo = _split_cols(w_rw, sizes)
    mu_r, mu_k, mu_v, mu_wlo, mu_alo, mu_glo = _split_cols(p['rwkv_mu'][layer].reshape(1, -1), sizes)
    has_vres = v_first is not None
    w_vlo = p['w_vres_in'][layer - 1] if has_vres else jnp.zeros((D_MODEL, RWKV_V_LORA), F32)
    lo_w = jnp.concatenate([_pad_to(w, cols=LANE) for w in (w_wlo, w_alo, w_glo, w_vlo)], axis=1)
    lo_mu = jnp.concatenate([_pad_to(m, cols=LANE) for m in (mu_wlo, mu_alo, mu_glo)]
                            + [jnp.zeros((1, LANE), F32)], axis=1)
    head_sum = (jnp.arange(WIDTH)[:, None] // RWKV_HEAD == jnp.arange(LANE)[None, :])
    consts = [g0, bf(w_r), bf(w_k), bf(w_v), bf(lo_w), mu_r, mu_k, mu_v, lo_mu,
              _row(p['rwkv_w0'][layer]), bf(_pad_to(p['rwkv_w_w2'][layer], rows=LANE)),
              _row(p['rwkv_a0'][layer]), bf(_pad_to(p['rwkv_a_w2'][layer], rows=LANE)),
              bf(_pad_to(p['rwkv_g_w2'][layer], rows=LANE)),
              _row(p['rwkv_k_k'][layer]), _row(p['rwkv_k_a'][layer]), _row(p['rwkv_r_k'][layer]),
              _row(p['rwkv_ln_g'][layer]), _row(p['rwkv_ln_b'][layer]), bf(head_sum)]
    vmem = lambda *shape: pltpu.VMEM(shape, F32)
    scratch = [vmem(SUBLANE, WIDTH)] * 4 + [vmem(RWKV_HEAD, WIDTH)]
    if has_vres:
        consts += [_row(p['rwkv_v0'][layer - 1]), bf(_pad_to(p['rwkv_v_w2'][layer - 1], rows=LANE))]
        y, = _mixer_call(_rwkv_kernel, f"rwkv_{layer}", x, consts, extra_in=[v_first],
                         scratch=scratch, has_vres=True)
        return y, v_first
    return tuple(_mixer_call(_rwkv_kernel, f"rwkv_{layer}", x, consts, n_out_f32=1,
                             scratch=scratch, has_vres=False))


def _merge(x2d, ys, p, layer, g0, w_gates):
    return _token_call(
        _merge_kernel, f"merge_{layer}", x2d, ys,
        [g0, _row(p['norm_g'][layer, 1]), w_gates.astype(BF16), p['w_branch'][layer].astype(BF16),
         p['w_out'][layer].astype(BF16)])


def _ffn(x2d, p, layer):
    return _token_call(
        _ffn_kernel, f"ffn_{layer}", x2d, [],
        [_row(p['norm_g'][layer, 2]), _row(p['norm_g'][layer, 3]), p['ffn_w_gate'][layer].astype(BF16),
         p['ffn_w_up'][layer].astype(BF16), p['ffn_w_down'][layer].astype(BF16)])


def kernel(x, norm_g, w_in, w_vres_in, ssd_conv_w, ssd_conv_b, ssd_dt_bias, ssd_a_log, ssd_d, ssd_norm_g,
           gla_alpha_w, gla_alpha_b, gla_norm_g, lru_conv_w, lru_conv_b, lru_wa, lru_ba, lru_wx, lru_bx,
           lru_lambda, rwkv_mu, rwkv_w0, rwkv_w_w2, rwkv_a0, rwkv_a_w2, rwkv_v0, rwkv_v_w2, rwkv_g_w2,
           rwkv_k_k, rwkv_k_a, rwkv_r_k, rwkv_ln_g, rwkv_ln_b, w_branch, w_out,
           ffn_w_gate, ffn_w_up, ffn_w_down):
    p = dict(locals())
    bsz, seq, _ = x.shape
    flat = lambda t: t.reshape(bsz * seq, t.shape[-1])
    v_first = None
    for layer in range(w_in.shape[0]):
        g0 = _row(norm_g[layer, 0])
        (w_sz, w_sxbc, w_sdt, w_gq, w_gk, w_gv, w_gr, w_glo, w_lx, w_lg, w_rw, w_gates) = _split_cols(
            w_in[layer], IN_SIZES)
        y_ssd = _ssd_mixer(x, p, layer, g0, w_sz, w_sxbc, w_sdt)
        y_gla = _gla_mixer(x, p, layer, g0, w_gq, w_gk, w_gv, w_gr, w_glo)
        y_lru = _lru_mixer(x, p, layer, g0, w_lx, w_lg)
        y_rwkv, v_first = _rwkv_mixer(x, p, layer, g0, w_rw, v_first)
        x2d = _merge(flat(x), [flat(y_ssd), flat(y_gla), flat(y_lru), flat(y_rwkv)], p, layer, g0, w_gates)
        x = _ffn(x2d, p, layer).reshape(bsz, seq, D_MODEL)
    return x
```

```python
import functools

import jax
import jax.numpy as jnp
from jax import lax
from jax.experimental import pallas as pl
from jax.experimental.pallas import tpu as pltpu

F32 = jnp.float32
BF16 = jnp.bfloat16

D_MODEL = 1024
WIDTH = 512
RMS_EPS = 1e-6
CONV_WIDTH = 4
SSD_HEADS = 8
SSD_HEADDIM = 64
SSD_STATE = 64
SSD_CHUNK = 128
SSD_XBC = WIDTH + 4 * SSD_STATE
GLA_HEADS = 4
GLA_DK = 64
GLA_DV = 128
GLA_LOWRANK = 16
GLA_TAU = 16.0
GLA_CHUNK = 64
LRU_C = 8.0
RWKV_HEADS = 8
RWKV_HEAD = 64
RWKV_CHUNK = 64
RWKV_GN_EPS = 64e-5
RWKV_LORAS = (32, 32, 96)
RWKV_V_LORA = 32
FFN_HIDDEN = 2816
FFN_SPLIT = 2
LANE = 128
SUBLANE = 8
TIME_TILE = 512
TOKEN_TILE = 512
VMEM_LIMIT_BYTES = 56 * 1024 * 1024

IN_SIZES = (WIDTH, SSD_XBC, SSD_HEADS,
            GLA_HEADS * GLA_DK, GLA_HEADS * GLA_DK, WIDTH, WIDTH, GLA_LOWRANK,
            WIDTH, WIDTH,
            3 * WIDTH + sum(RWKV_LORAS),
            4 * D_MODEL)
N_IN = sum(IN_SIZES)


def _rmsn(x):
    return x * lax.rsqrt(jnp.mean(x * x, axis=-1, keepdims=True) + RMS_EPS)


def _dot(a, b):
    return jnp.dot(a.astype(BF16), b.astype(BF16), preferred_element_type=F32)


def _dot_nt(a, b):
    return lax.dot_general(a.astype(BF16), b.astype(BF16), (((1,), (1,)), ((), ())),
                           preferred_element_type=F32)


def _dot_tn(a, b):
    return lax.dot_general(a.astype(BF16), b.astype(BF16), (((0,), (0,)), ((), ())),
                           preferred_element_type=F32)


def _split(x, parts):
    out = []
    for _ in range(parts - 1):
        hi = x.astype(BF16)
        out.append(hi)
        x = x - hi.astype(F32)
    return out + [x.astype(BF16)]


def _sel_dot(sel, x, parts=3):
    s = sel.astype(BF16)
    return sum(jnp.dot(s, p, preferred_element_type=F32) for p in _split(x, parts))


def _dot_sel(x, sel, parts=3):
    s = sel.astype(BF16)
    return sum(jnp.dot(p, s, preferred_element_type=F32) for p in _split(x, parts))


def _iota(shape, axis):
    return lax.broadcasted_iota(jnp.int32, shape, axis)


def _tril(n, m=None, strict=False):
    m = n if m is None else m
    r, c = _iota((n, m), 0), _iota((n, m), 1)
    return (r > c) if strict else (r >= c)


def _shift_rows(x, tail, j):
    tt, width = x.shape
    r3 = pltpu.roll(x.reshape(tt // SUBLANE, SUBLANE, width), j, 1)
    prev = jnp.concatenate([pltpu.roll(tail, j, 0)[None], r3[:-1]], axis=0)
    return jnp.where(_iota(r3.shape, 1) < j, prev, r3).reshape(tt, width)


def _causal_conv(x, tail, w, b):
    y = x * w[CONV_WIDTH - 1:CONV_WIDTH] + b
    for j in range(1, CONV_WIDTH):
        y = y + _shift_rows(x, tail, j) * w[CONV_WIDTH - 1 - j:CONV_WIDTH - j]
    return y


def _expand_heads(v, n_heads, width):
    rows = v.shape[0]
    return jnp.concatenate(
        [jnp.broadcast_to(v[:, h:h + 1], (rows, width)) for h in range(n_heads)], axis=1)


def _sigmoid(x):
    return jax.nn.sigmoid(x)


def _silu(x):
    return x * jax.nn.sigmoid(x)


def _normed_input(x_ref, g_ref):
    return (_rmsn(x_ref[0]) * g_ref[...]).astype(BF16)


def _ssd_body(h, wz_ref, wxbc_ref, wdt_ref, cw_ref, cb_ref, dtb_ref, alog_ref,
              dsk_ref, ng_ref, o_ref, tail_ref, st_ref, *, tt):
    c_len = SSD_CHUNK
    n_st = 2 * SSD_STATE
    z = jnp.dot(h, wz_ref[...], preferred_element_type=F32)
    yield
    xbc_raw = jnp.dot(h, wxbc_ref[...], preferred_element_type=F32)
    dt_raw = jnp.dot(h, wdt_ref[...], preferred_element_type=F32)
    yield
    xbc = _silu(_causal_conv(xbc_raw, tail_ref[...], cw_ref[...], cb_ref[...]))
    tail_ref[...] = xbc_raw[tt - SUBLANE:]
    dt = jax.nn.softplus(dt_raw + dtb_ref[...])
    da = dt * (-jnp.exp(alog_ref[...]))
    yield

    ltri = _tril(c_len).astype(F32)
    tril = _tril(c_len)
    lane = _iota((c_len, LANE), 1)
    low = lane < SSD_HEADDIM
    st_mask = (_iota((n_st, WIDTH), 0) < SSD_STATE) == (_iota((n_st, WIDTH), 1) < WIDTH // 2)
    chunks = range(tt // c_len)
    rows = lambda t, c: t[c * c_len:(c + 1) * c_len]
    expand = lambda t: _expand_heads(t, SSD_HEADS, SSD_HEADDIM)
    xs = [rows(xbc, c)[:, :WIDTH] for c in chunks]
    bm = [rows(xbc, c)[:, WIDTH:WIDTH + n_st] for c in chunks]
    cm = [rows(xbc, c)[:, WIDTH + n_st:] for c in chunks]
    cum = [_sel_dot(ltri, rows(da, c)) for c in chunks]
    cum_t = [t.T for t in cum]
    dt_t = [rows(dt, c).T for c in chunks]
    yield
    cb = [(_dot_nt(jnp.where(low, cm[c], 0.0), bm[c]), _dot_nt(jnp.where(low, 0.0, cm[c]), bm[c]))
          for c in chunks]
    yield
    upd = [_dot_tn(bm[c], xs[c] * expand(jnp.exp(cum[c][c_len - 1:c_len] - cum[c]) * rows(dt, c)))
           for c in chunks]
    yield
    intra = []
    for c in chunks:
        pairs = []
        for hp in range(SSD_HEADS // 2):
            ws = []
            for hd in (2 * hp, 2 * hp + 1):
                seg = cum[c][:, hd:hd + 1] - cum_t[c][hd:hd + 1, :]
                m = jnp.exp(jnp.where(tril, seg, -jnp.inf)) * dt_t[c][hd:hd + 1, :]
                ws.append((cb[c][hp // 2] * m).astype(BF16))
            xp = xs[c][:, hp * LANE:(hp + 1) * LANE]
            rhs = jnp.concatenate([jnp.where(low, xp, 0.0), jnp.where(low, 0.0, xp)], axis=0)
            pairs.append(_dot(jnp.concatenate(ws, axis=1), rhs))
        intra.append(jnp.concatenate(pairs, axis=1))
        yield
    sts = [st_ref[...]]
    for c in chunks:
        sts.append(sts[c] * expand(jnp.exp(cum[c][c_len - 1:c_len])) + jnp.where(st_mask, upd[c], 0.0))
    st_ref[...] = sts[-1]
    yield
    ys = [intra[c] + _dot(cm[c], sts[c]) * expand(jnp.exp(cum[c])) + dsk_ref[...] * xs[c] for c in chunks]
    yield
    y = jnp.concatenate(ys, axis=0) * _silu(z)
    half = WIDTH // 2
    y = jnp.concatenate([_rmsn(y[:, :half]), _rmsn(y[:, half:])], axis=1)
    o_ref[0] = (y * ng_ref[...]).astype(o_ref.dtype)


def _gla_body(h, wq_ref, wk_ref, wv_ref, wr_ref, wlo_ref, aw_ref, ab_ref, ng_ref,
              o_ref, st_ref, *, tt):
    c_len = GLA_CHUNK
    qk = GLA_HEADS * GLA_DK
    q = jnp.dot(h, wq_ref[...], preferred_element_type=F32) * (GLA_DK ** -0.5)
    k = jnp.dot(h, wk_ref[...], preferred_element_type=F32)
    yield
    v = jnp.dot(h, wv_ref[...], preferred_element_type=F32)
    yield
    r = jnp.dot(h, wr_ref[...], preferred_element_type=F32)
    yield
    lo = jnp.dot(h, wlo_ref[...], preferred_element_type=F32)
    log_a = jax.nn.log_sigmoid(_dot(lo, aw_ref[...]) + ab_ref[...]) / GLA_TAU
    yield

    ltri = _tril(c_len).astype(F32)
    tril_stack = (_iota((GLA_HEADS * c_len, c_len), 0) & (c_len - 1)) >= _iota((GLA_HEADS * c_len, c_len), 1)
    head_of_lane = _iota((c_len, qk), 1) >> 6
    head_of_lane_dv = _iota((GLA_DV, qk), 1) >> 6
    chunks = range(tt // c_len)
    heads = range(GLA_HEADS)
    rows = lambda t, c: t[c * c_len:(c + 1) * c_len]
    stack = lambda t: jnp.concatenate([jnp.where(head_of_lane == hd, t, 0.0) for hd in heads], axis=0)
    b = [_sel_dot(ltri, rows(log_a, c)) for c in chunks]
    ref = [t[c_len // 2:c_len // 2 + 1] for t in b]
    b_last = [t[c_len - 1:c_len] for t in b]
    yield
    att = [jnp.where(tril_stack,
                     _dot_nt(stack(rows(q, c) * jnp.exp(b[c] - ref[c])), rows(k, c) * jnp.exp(ref[c] - b[c])),
                     0.0) for c in chunks]
    yield
    full = [_dot_tn(rows(v, c), rows(k, c) * jnp.exp(b_last[c] - b[c])) for c in chunks]
    yield
    sts = [st_ref[...]]
    for c in chunks:
        upd = sum(jnp.where(head_of_lane_dv == hd, full[c][hd * GLA_DV:(hd + 1) * GLA_DV], 0.0)
                  for hd in heads)
        sts.append(sts[c] * jnp.exp(b_last[c]) + upd)
    st_ref[...] = sts[-1]
    yield
    inter = [_dot_nt(stack(rows(q, c) * jnp.exp(b[c])), sts[c]) for c in chunks]
    yield
    ys = [jnp.concatenate(
        [_dot(att[c][hd * c_len:(hd + 1) * c_len], rows(v, c)[:, hd * GLA_DV:(hd + 1) * GLA_DV])
         + inter[c][hd * c_len:(hd + 1) * c_len] for hd in heads], axis=1) for c in chunks]
    yield
    y = jnp.concatenate(ys, axis=0)
    o = jnp.concatenate(
        [_rmsn(y[:, hd * GLA_DV:(hd + 1) * GLA_DV]) * ng_ref[...] for hd in range(GLA_HEADS)], axis=1)
    o_ref[0] = (o * _silu(r)).astype(o_ref.dtype)


def _lru_body(h, wx_ref, wg_ref, cw_ref, cb_ref, wa_ref, ba_ref, wi_ref, bi_ref,
              lam_ref, o_ref, tail_ref, carry_ref, *, tt):
    xr = jnp.dot(h, wx_ref[...], preferred_element_type=F32)
    yield
    gate = jnp.dot(h, wg_ref[...], preferred_element_type=F32)
    yield
    xb = _causal_conv(xr, tail_ref[...], cw_ref[...], cb_ref[...])
    tail_ref[...] = xr[tt - SUBLANE:]
    yield
    r = _sigmoid(_dot(xb, wa_ref[...]) + ba_ref[...])
    yield
    i = _sigmoid(_dot(xb, wi_ref[...]) + bi_ref[...])
    yield
    log_a = -LRU_C * r * jax.nn.softplus(-lam_ref[...])
    a = jnp.exp(log_a)
    u = jnp.sqrt(1.0 - jnp.exp(2.0 * log_a)) * (i * xb)
    yield

    n_blk = tt // SUBLANE
    a = a.reshape(n_blk, SUBLANE, WIDTH)
    u = u.reshape(n_blk, SUBLANE, WIDTH)
    rid = _iota(a.shape, 1)
    s = 1
    while s < SUBLANE:
        m = rid >= s
        u = jnp.where(m, a * pltpu.roll(u, s, 1) + u, u)
        a = jnp.where(m, a * pltpu.roll(a, s, 1), a)
        s *= 2
        yield
    carry = carry_ref[...]
    blocks = []
    for blk in range(n_blk):
        hb = u[blk] + a[blk] * carry
        blocks.append(hb)
        carry = jnp.broadcast_to(hb[SUBLANE - 1:], (SUBLANE, WIDTH))
        if blk % 16 == 15:
            yield
    carry_ref[...] = carry
    hs = jnp.concatenate(blocks, axis=0)
    o_ref[0] = (hs * jax.nn.gelu(gate, approximate=True)).astype(o_ref.dtype)


def _rwkv_body(h, consts, vres, o_ref, vfo_ref, scratch, *, tt):
    (wr_ref, wk_ref, wv_ref, wlo_ref, mur_ref, muk_ref, muv_ref, mulo_ref,
     w0_ref, ww2_ref, a0_ref, aw2_ref, gw2_ref, kk_ref, ka_ref, rk_ref, lng_ref, lnb_ref,
     hsum_ref) = consts
    has_vres = vres is not None
    if has_vres:
        v0_ref, vw2_ref, vf_ref = vres
    (tr_ref, tk_ref, tv_ref, tlo_ref, s_ref) = scratch
    c_len = RWKV_CHUNK
    n = RWKV_HEAD

    def shifted(w_ref, tail_ref, mu_ref):
        p = jnp.dot(h, w_ref[...], preferred_element_type=F32)
        prev = _shift_rows(p, tail_ref[...], 1)
        tail_ref[...] = p[tt - SUBLANE:]
        return p + (prev - p) * mu_ref[...]

    r = shifted(wr_ref, tr_ref, mur_ref)
    yield
    k = shifted(wk_ref, tk_ref, muk_ref)
    yield
    v = shifted(wv_ref, tv_ref, muv_ref)
    yield
    lo = shifted(wlo_ref, tlo_ref, mulo_ref)
    yield
    w_lo, a_lo, g_lo = lo[:, :LANE], lo[:, LANE:2 * LANE], lo[:, 2 * LANE:3 * LANE]
    w_log = -jax.nn.softplus(-(w0_ref[...] + _dot(jnp.tanh(w_lo), ww2_ref[...]))) - 0.5
    log_w = -jnp.exp(w_log)
    a = _sigmoid(a0_ref[...] + _dot(a_lo, aw2_ref[...]))
    g = _dot(_sigmoid(g_lo), gw2_ref[...])
    yield
    if has_vres:
        v = v + (vf_ref[0] - v) * _sigmoid(v0_ref[...] + _dot(lo[:, 3 * LANE:], vw2_ref[...]))
    else:
        vfo_ref[0] = v
    hsum = hsum_ref[...]
    head_sum = lambda t: _expand_heads(_dot_sel(t, hsum, parts=2), RWKV_HEADS, n)
    kk = k * kk_ref[...]
    kk = kk / jnp.maximum(jnp.sqrt(head_sum(kk * kk)), 1e-12)
    k = k * (1.0 + (a - 1.0) * ka_ref[...])
    bonus = head_sum(r * k * rk_ref[...]) * v
    yield

    n_chunks = tt // c_len
    n_pairs = WIDTH // LANE
    ltri = _tril(c_len).astype(F32)
    lc = jnp.concatenate(
        [_sel_dot(ltri, log_w[c * c_len:(c + 1) * c_len]) for c in range(n_chunks)], axis=0)
    yield
    e_pos = jnp.exp(lc)
    e_neg = jnp.exp(-lc)
    rt = r * e_pos
    at = -kk * jnp.exp(lc - log_w)
    bt = kk * a * e_neg
    kt = k * e_neg
    yield

    row = _iota((c_len, LANE), 0)
    col = _iota((c_len, LANE), 1) & (n - 1)
    stril = row > col
    eye = (row == col).astype(F32)
    tril2 = _iota((c_len, 2 * LANE), 0) >= (_iota((c_len, 2 * LANE), 1) & (n - 1))
    low = _iota((c_len, LANE), 1) < n

    def bd(t):
        return jnp.concatenate([jnp.where(low, t, 0.0), jnp.where(low, 0.0, t)], axis=0)

    blocks = [(c, q) for c in range(n_chunks) for q in range(n_pairs)]
    blk = lambda t, cq: t[cq[0] * c_len:(cq[0] + 1) * c_len, cq[1] * LANE:(cq[1] + 1) * LANE]

    e_last_t = jnp.concatenate(
        [e_pos[(c + 1) * c_len - 1:(c + 1) * c_len] for c in range(n_chunks)], axis=0).T

    def decay_cols(cq):
        c, q = cq
        col_of = lambda hd: jnp.broadcast_to(e_last_t[hd * n:(hd + 1) * n, c:c + 1], (c_len, LANE))
        return jnp.where(low, col_of(2 * q), col_of(2 * q + 1))

    vbd = [bd(blk(v, cq)) for cq in blocks]
    yield
    m = [_dot_nt(jnp.concatenate([blk(at, cq), blk(rt, cq), eye], axis=0),
                 jnp.concatenate([bd(blk(bt, cq)), bd(blk(kt, cq))], axis=0)) for cq in blocks]
    yield
    p = [jnp.where(stril, mb[:c_len, :LANE], 0.0) for mb in m]
    rbk = [jnp.concatenate([jnp.where(tril2, mb[c_len:2 * c_len], 0.0), mb[2 * c_len:]], axis=0) for mb in m]
    lakv = [_dot(jnp.where(stril, mb[:c_len, LANE:], 0.0), vb) for mb, vb in zip(m, vbd)]
    yield
    inv = [eye + pb for pb in p]
    span = 2
    while span < c_len:
        p = [_dot(pb, bd(pb)) for pb in p]
        yield
        inv = [ib + _dot(ib, bd(pb)) for ib, pb in zip(inv, p)]
        yield
        span *= 2
    tz = [_dot(ib, jnp.concatenate([bd(blk(at, cq)), bd(lb)], axis=1))
          for ib, lb, cq in zip(inv, lakv, blocks)]
    ar = [jnp.concatenate([tzb[:, :LANE], blk(rt, cq)], axis=0) for tzb, cq in zip(tz, blocks)]
    yield
    e_cols = [decay_cols(cq) for cq in blocks]
    yield

    st = [s_ref[:, q * LANE:(q + 1) * LANE] for q in range(n_pairs)]
    y_rows = []
    for c in range(n_chunks):
        ids = range(c * n_pairs, (c + 1) * n_pairs)
        gs = [_dot(ar[i], bd(st[q])) for q, i in enumerate(ids)]
        yield
        uv = [jnp.concatenate([bd(gs[q][:c_len] + tz[i][:, LANE:]), vbd[i]], axis=0) for q, i in enumerate(ids)]
        yd = [_dot(rbk[i], uv[q]) for q, i in enumerate(ids)]
        y_rows.append(jnp.concatenate([gs[q][c_len:] + yd[q][:c_len] for q in range(n_pairs)], axis=1))
        st = [(st[q] + yd[q][c_len:]) * e_cols[i] for q, i in enumerate(ids)]
        yield
    s_ref[...] = jnp.concatenate(st, axis=1)

    y = jnp.concatenate(y_rows, axis=0)
    mean = head_sum(y) * (1.0 / n)
    yield
    d = y - mean
    var = head_sum(d * d) * (1.0 / n)
    yield
    y = d * lax.rsqrt(var + RWKV_GN_EPS) * lng_ref[...] + lnb_ref[...] + bonus
    o_ref[0] = (y * g).astype(o_ref.dtype)


N_SSD_CONSTS, N_GLA_CONSTS, N_LRU_CONSTS, N_RWKV_CONSTS = 9, 8, 9, 19
N_SSD_SCRATCH, N_GLA_SCRATCH, N_LRU_SCRATCH, N_RWKV_SCRATCH = 2, 1, 2, 5
MIX_STEPS = (4, 1, 1, 1)
MIX_DELAY = (0, 0, 0, 6)


def _interleave(gens):
    live = list(gens)
    rnd = 0
    while live:
        nxt = []
        for gen, steps, delay in live:
            if rnd < delay:
                nxt.append((gen, steps, delay))
                continue
            for _ in range(steps):
                if next(gen, StopIteration) is StopIteration:
                    break
            else:
                nxt.append((gen, steps, delay))
        live = nxt
        rnd += 1


def _mixers_kernel(*refs, tt, has_vres):
    refs = list(refs)
    take = lambda k: [refs.pop(0) for _ in range(k)]
    x_ref, g_ref = take(2)
    ssd_c, gla_c, lru_c, rwkv_c = (take(k) for k in (N_SSD_CONSTS, N_GLA_CONSTS, N_LRU_CONSTS, N_RWKV_CONSTS))
    vres = take(3) if has_vres else None
    y_ssd, y_gla, y_lru, y_rwkv = take(4)
    vfo_ref = None if has_vres else take(1)[0]
    ssd_s, gla_s, lru_s, rwkv_s = (take(k) for k in (N_SSD_SCRATCH, N_GLA_SCRATCH, N_LRU_SCRATCH, N_RWKV_SCRATCH))

    @pl.when(pl.program_id(1) == 0)
    def _():
        for ref in ssd_s + gla_s + lru_s + rwkv_s:
            ref[...] = jnp.zeros_like(ref)

    h = _normed_input(x_ref, g_ref)
    _interleave([(_rwkv_body(h, rwkv_c, vres, y_rwkv, vfo_ref, rwkv_s, tt=tt), MIX_STEPS[0], MIX_DELAY[0]),
                 (_ssd_body(h, *ssd_c, y_ssd, *ssd_s, tt=tt), MIX_STEPS[1], MIX_DELAY[1]),
                 (_gla_body(h, *gla_c, y_gla, *gla_s, tt=tt), MIX_STEPS[2], MIX_DELAY[2]),
                 (_lru_body(h, *lru_c, y_lru, *lru_s, tt=tt), MIX_STEPS[3], MIX_DELAY[3])])


def _merge_kernel(x_ref, y0_ref, y1_ref, y2_ref, y3_ref, g0_ref, g1_ref, wgate_ref, wbr_ref, wout_ref,
                  o_ref):
    x = x_ref[...]
    h = (_rmsn(x) * g0_ref[...]).astype(BF16)
    acc = jnp.zeros(x.shape, F32)
    for kb, y_ref in enumerate((y0_ref, y1_ref, y2_ref, y3_ref)):
        logits = jnp.dot(h, wgate_ref[:, kb * D_MODEL:(kb + 1) * D_MODEL], preferred_element_type=F32)
        branch = jnp.dot(y_ref[...], wbr_ref[kb], preferred_element_type=F32)
        acc = acc + _sigmoid(logits) * branch
    o = jnp.dot(acc.astype(BF16), wout_ref[...], preferred_element_type=F32)
    o_ref[...] = x + _rmsn(o) * g1_ref[...]


def _ffn_kernel(x_ref, g2_ref, g3_ref, wg_ref, wu_ref, wd_ref, o_ref):
    x = x_ref[...]
    h = (_rmsn(x) * g2_ref[...]).astype(BF16)
    f = jnp.zeros(x.shape, F32)
    step = FFN_HIDDEN // FFN_SPLIT
    for c in range(FFN_SPLIT):
        cols = slice(c * step, (c + 1) * step)
        gt = jnp.dot(h, wg_ref[:, cols], preferred_element_type=F32)
        up = jnp.dot(h, wu_ref[:, cols], preferred_element_type=F32)
        f = f + jnp.dot((_silu(gt) * up).astype(BF16), wd_ref[cols, :], preferred_element_type=F32)
    o_ref[...] = x + _rmsn(f) * g3_ref[...]


def _const_spec(a):
    nd = a.ndim
    return pl.BlockSpec(a.shape, lambda *_: (0,) * nd, pipeline_mode=pl.Buffered(1))


def _time_tile(seq):
    return min(TIME_TILE, seq)


def _mixers_call(name, x, consts, v_first, scratch):
    bsz, seq, _ = x.shape
    tt = _time_tile(seq)
    assert seq % tt == 0 and tt % SSD_CHUNK == 0
    has_vres = v_first is not None
    tile = lambda width: pl.BlockSpec((1, tt, width), lambda b, t: (b, t, 0))
    tiles_in = [x] + ([v_first] if has_vres else [])
    in_specs = [tile(D_MODEL)] + [_const_spec(c) for c in consts] + ([tile(WIDTH)] if has_vres else [])
    out_shape = [jax.ShapeDtypeStruct((bsz, seq, WIDTH), BF16)] * 4
    if not has_vres:
        out_shape.append(jax.ShapeDtypeStruct((bsz, seq, WIDTH), F32))
    return pl.pallas_call(
        functools.partial(_mixers_kernel, tt=tt, has_vres=has_vres),
        grid=(bsz, seq // tt),
        in_specs=in_specs,
        out_specs=[tile(WIDTH)] * len(out_shape),
        out_shape=out_shape,
        scratch_shapes=list(scratch),
        compiler_params=pltpu.CompilerParams(
            dimension_semantics=("parallel", "arbitrary"), vmem_limit_bytes=VMEM_LIMIT_BYTES),
        name=name,
    )(tiles_in[0], *consts, *tiles_in[1:])


def _token_call(body, name, x2d, tiles, consts):
    tokens = x2d.shape[0]
    tm = min(TOKEN_TILE, tokens)
    assert tokens % tm == 0
    tile = lambda width: pl.BlockSpec((tm, width), lambda i: (i, 0))
    return pl.pallas_call(
        body,
        grid=(tokens // tm,),
        in_specs=[tile(D_MODEL)] + [tile(t.shape[-1]) for t in tiles] + [_const_spec(c) for c in consts],
        out_specs=tile(D_MODEL),
        out_shape=jax.ShapeDtypeStruct(x2d.shape, F32),
        compiler_params=pltpu.CompilerParams(
            dimension_semantics=("parallel",), vmem_limit_bytes=VMEM_LIMIT_BYTES),
        name=name,
    )(x2d, *tiles, *consts)


def _row(v, width=None):
    v = v.reshape(1, -1).astype(F32)
    if width is not None and v.shape[1] < width:
        v = jnp.pad(v, ((0, 0), (0, width - v.shape[1])))
    return v


def _pad_to(w, rows=None, cols=None):
    rows = w.shape[0] if rows is None else rows
    cols = w.shape[1] if cols is None else cols
    return jnp.pad(w, ((0, rows - w.shape[0]), (0, cols - w.shape[1])))


def _split_cols(w, sizes):
    out, start = [], 0
    for nsz in sizes:
        out.append(w[:, start:start + nsz])
        start += nsz
    return out


def _block_diag(w):
    nb, bi, bj = w.shape
    eye = jnp.eye(nb, dtype=w.dtype)
    return jnp.einsum('hij,hg->higj', w, eye).reshape(nb * bi, nb * bj)


def _vmem(*shape):
    return pltpu.VMEM(shape, F32)


def _ssd_operands(p, layer, w_z, w_xbc, w_dt):
    consts = [w_z.astype(BF16), w_xbc.astype(BF16), _pad_to(w_dt, cols=LANE).astype(BF16),
              p['ssd_conv_w'][layer], _row(p['ssd_conv_b'][layer]), _row(p['ssd_dt_bias'][layer], LANE),
              _row(p['ssd_a_log'][layer], LANE), _row(jnp.repeat(p['ssd_d'][layer], SSD_HEADDIM)),
              _row(p['ssd_norm_g'][layer])]
    return consts, [_vmem(SUBLANE, SSD_XBC), _vmem(2 * SSD_STATE, WIDTH)]


def _gla_operands(p, layer, w_q, w_k, w_v, w_r, w_lo):
    consts = [w_q.astype(BF16), w_k.astype(BF16), w_v.astype(BF16), w_r.astype(BF16),
              _pad_to(w_lo, cols=LANE).astype(BF16), _pad_to(p['gla_alpha_w'][layer], rows=LANE).astype(BF16),
              _row(p['gla_alpha_b'][layer]), _row(p['gla_norm_g'][layer])]
    return consts, [_vmem(GLA_DV, GLA_HEADS * GLA_DK)]


def _lru_operands(p, layer, w_x, w_gate):
    consts = [w_x.astype(BF16), w_gate.astype(BF16), p['lru_conv_w'][layer], _row(p['lru_conv_b'][layer]),
              _block_diag(p['lru_wa'][layer]).astype(BF16), _row(p['lru_ba'][layer]),
              _block_diag(p['lru_wx'][layer]).astype(BF16), _row(p['lru_bx'][layer]),
              _row(p['lru_lambda'][layer])]
    return consts, [_vmem(SUBLANE, WIDTH)] * 2


def _rwkv_operands(p, layer, w_rw, has_vres):
    bf = lambda w: w.astype(BF16)
    sizes = (WIDTH, WIDTH, WIDTH) + RWKV_LORAS
    w_r, w_k, w_v, w_wlo, w_alo, w_glo = _split_cols(w_rw, sizes)
    mu_r, mu_k, mu_v, mu_wlo, mu_alo, mu_glo = _split_cols(p['rwkv_mu'][layer].reshape(1, -1), sizes)
    w_vlo = p['w_vres_in'][layer - 1] if has_vres else jnp.zeros((D_MODEL, RWKV_V_LORA), F32)
    lo_w = jnp.concatenate([_pad_to(w, cols=LANE) for w in (w_wlo, w_alo, w_glo, w_vlo)], axis=1)
    lo_mu = jnp.concatenate([_pad_to(m, cols=LANE) for m in (mu_wlo, mu_alo, mu_glo)]
                            + [jnp.zeros((1, LANE), F32)], axis=1)
    head_sum = (jnp.arange(WIDTH)[:, None] // RWKV_HEAD == jnp.arange(LANE)[None, :])
    consts = [bf(w_r), bf(w_k), bf(w_v), bf(lo_w), mu_r, mu_k, mu_v, lo_mu,
              _row(p['rwkv_w0'][layer]), bf(_pad_to(p['rwkv_w_w2'][layer], rows=LANE)),
              _row(p['rwkv_a0'][layer]), bf(_pad_to(p['rwkv_a_w2'][layer], rows=LANE)),
              bf(_pad_to(p['rwkv_g_w2'][layer], rows=LANE)),
              _row(p['rwkv_k_k'][layer]), _row(p['rwkv_k_a'][layer]), _row(p['rwkv_r_k'][layer]),
              _row(p['rwkv_ln_g'][layer]), _row(p['rwkv_ln_b'][layer]), bf(head_sum)]
    if has_vres:
        consts += [_row(p['rwkv_v0'][layer - 1]), bf(_pad_to(p['rwkv_v_w2'][layer - 1], rows=LANE))]
    return consts, [_vmem(SUBLANE, WIDTH)] * 4 + [_vmem(RWKV_HEAD, WIDTH)]


def _mixers(x, p, layer, g0, w_cols, v_first):
    (w_sz, w_sxbc, w_sdt, w_gq, w_gk, w_gv, w_gr, w_glo, w_lx, w_lg, w_rw) = w_cols
    parts = [_ssd_operands(p, layer, w_sz, w_sxbc, w_sdt),
             _gla_operands(p, layer, w_gq, w_gk, w_gv, w_gr, w_glo),
             _lru_operands(p, layer, w_lx, w_lg),
             _rwkv_operands(p, layer, w_rw, v_first is not None)]
    consts = [g0] + [c for cs, _ in parts for c in cs]
    scratch = [s for _, ss in parts for s in ss]
    outs = _mixers_call(f"mixers_{layer}", x, consts, v_first, scratch)
    return list(outs[:4]), (v_first if v_first is not None else outs[4])


def _merge(x2d, ys, p, layer, g0, w_gates):
    return _token_call(
        _merge_kernel, f"merge_{layer}", x2d, ys,
        [g0, _row(p['norm_g'][layer, 1]), w_gates.astype(BF16), p['w_branch'][layer].astype(BF16),
         p['w_out'][layer].astype(BF16)])


def _ffn(x2d, p, layer):
    return _token_call(
        _ffn_kernel, f"ffn_{layer}", x2d, [],
        [_row(p['norm_g'][layer, 2]), _row(p['norm_g'][layer, 3]), p['ffn_w_gate'][layer].astype(BF16),
         p['ffn_w_up'][layer].astype(BF16), p['ffn_w_down'][layer].astype(BF16)])


def kernel(x, norm_g, w_in, w_vres_in, ssd_conv_w, ssd_conv_b, ssd_dt_bias, ssd_a_log, ssd_d, ssd_norm_g,
           gla_alpha_w, gla_alpha_b, gla_norm_g, lru_conv_w, lru_conv_b, lru_wa, lru_ba, lru_wx, lru_bx,
           lru_lambda, rwkv_mu, rwkv_w0, rwkv_w_w2, rwkv_a0, rwkv_a_w2, rwkv_v0, rwkv_v_w2, rwkv_g_w2,
           rwkv_k_k, rwkv_k_a, rwkv_r_k, rwkv_ln_g, rwkv_ln_b, w_branch, w_out,
           ffn_w_gate, ffn_w_up, ffn_w_down):
    p = dict(locals())
    bsz, seq, _ = x.shape
    flat = lambda t: t.reshape(bsz * seq, t.shape[-1])
    v_first = None
    for layer in range(w_in.shape[0]):
        g0 = _row(norm_g[layer, 0])
        *w_cols, w_gates = _split_cols(w_in[layer], IN_SIZES)
        ys, v_first = _mixers(x, p, layer, g0, w_cols, v_first)
        x2d = _merge(flat(x), [flat(y) for y in ys], p, layer, g0, w_gates)
        x = _ffn(x2d, p, layer).reshape(bsz, seq, D_MODEL)
    return x
```

```python
import functools

import jax
import jax.numpy as jnp
from jax import lax
from jax.experimental import pallas as pl
from jax.experimental.pallas import tpu as pltpu

F32 = jnp.float32
BF16 = jnp.bfloat16

D_MODEL = 1024
WIDTH = 512
RMS_EPS = 1e-6
CONV_WIDTH = 4
SSD_HEADS = 8
SSD_HEADDIM = 64
SSD_STATE = 64
SSD_CHUNK = 128
SSD_XBC = WIDTH + 4 * SSD_STATE
GLA_HEADS = 4
GLA_DK = 64
GLA_DV = 128
GLA_LOWRANK = 16
GLA_TAU = 16.0
GLA_CHUNK = 64
LRU_C = 8.0
RWKV_HEADS = 8
RWKV_HEAD = 64
RWKV_CHUNK = 64
RWKV_GN_EPS = 64e-5
RWKV_LORAS = (32, 32, 96)
RWKV_V_LORA = 32
FFN_HIDDEN = 2816
FFN_SPLIT = 2
LANE = 128
SUBLANE = 8
TIME_TILE = 512
TOKEN_TILE = 1024
SUB_TILES = 2
VMEM_LIMIT_BYTES = 56 * 1024 * 1024

IN_SIZES = (WIDTH, SSD_XBC, SSD_HEADS,
            GLA_HEADS * GLA_DK, GLA_HEADS * GLA_DK, WIDTH, WIDTH, GLA_LOWRANK,
            WIDTH, WIDTH,
            3 * WIDTH + sum(RWKV_LORAS),
            4 * D_MODEL)
N_IN = sum(IN_SIZES)


def _rmsn(x):
    return x * lax.rsqrt(jnp.mean(x * x, axis=-1, keepdims=True) + RMS_EPS)


def _dot(a, b):
    return jnp.dot(a.astype(BF16), b.astype(BF16), preferred_element_type=F32)


def _dot_nt(a, b):
    return lax.dot_general(a.astype(BF16), b.astype(BF16), (((1,), (1,)), ((), ())),
                           preferred_element_type=F32)


def _dot_tn(a, b):
    return lax.dot_general(a.astype(BF16), b.astype(BF16), (((0,), (0,)), ((), ())),
                           preferred_element_type=F32)


def _split(x, parts):
    out = []
    for _ in range(parts - 1):
        hi = x.astype(BF16)
        out.append(hi)
        x = x - hi.astype(F32)
    return out + [x.astype(BF16)]


def _sel_dot(sel, x, parts=3):
    s = sel.astype(BF16)
    return sum(jnp.dot(s, p, preferred_element_type=F32) for p in _split(x, parts))


def _dot_sel(x, sel, parts=3):
    s = sel.astype(BF16)
    return sum(jnp.dot(p, s, preferred_element_type=F32) for p in _split(x, parts))


def _iota(shape, axis):
    return lax.broadcasted_iota(jnp.int32, shape, axis)


def _tril(n, m=None, strict=False):
    m = n if m is None else m
    r, c = _iota((n, m), 0), _iota((n, m), 1)
    return (r > c) if strict else (r >= c)


def _shift_rows(x, tail, j):
    tt, width = x.shape
    r3 = pltpu.roll(x.reshape(tt // SUBLANE, SUBLANE, width), j, 1)
    prev = jnp.concatenate([pltpu.roll(tail, j, 0)[None], r3[:-1]], axis=0)
    return jnp.where(_iota(r3.shape, 1) < j, prev, r3).reshape(tt, width)


def _causal_conv(x, tail, w, b):
    y = x * w[CONV_WIDTH - 1:CONV_WIDTH] + b
    for j in range(1, CONV_WIDTH):
        y = y + _shift_rows(x, tail, j) * w[CONV_WIDTH - 1 - j:CONV_WIDTH - j]
    return y


def _expand_heads(v, n_heads, width):
    rows = v.shape[0]
    return jnp.concatenate(
        [jnp.broadcast_to(v[:, h:h + 1], (rows, width)) for h in range(n_heads)], axis=1)


def _sigmoid(x):
    return jax.nn.sigmoid(x)


def _silu(x):
    return x * jax.nn.sigmoid(x)


def _normed_input(x_ref, g_ref):
    return (_rmsn(x_ref[0]) * g_ref[...]).astype(BF16)


def _ssd_body(h, wz_ref, wxbc_ref, wdt_ref, cw_ref, cb_ref, dtb_ref, alog_ref,
              dsk_ref, ng_ref, o_ref, tail_ref, st_ref, *, tt):
    c_len = SSD_CHUNK
    n_st = 2 * SSD_STATE
    z = jnp.dot(h, wz_ref[...], preferred_element_type=F32)
    yield
    xbc_raw = jnp.dot(h, wxbc_ref[...], preferred_element_type=F32)
    dt_raw = jnp.dot(h, wdt_ref[...], preferred_element_type=F32)
    yield
    xbc = _silu(_causal_conv(xbc_raw, tail_ref[...], cw_ref[...], cb_ref[...]))
    tail_ref[...] = xbc_raw[tt - SUBLANE:]
    dt = jax.nn.softplus(dt_raw + dtb_ref[...])
    da = dt * (-jnp.exp(alog_ref[...]))
    yield

    ltri = _tril(c_len).astype(F32)
    tril = _tril(c_len)
    lane = _iota((c_len, LANE), 1)
    low = lane < SSD_HEADDIM
    st_mask = (_iota((n_st, WIDTH), 0) < SSD_STATE) == (_iota((n_st, WIDTH), 1) < WIDTH // 2)
    chunks = range(tt // c_len)
    rows = lambda t, c: t[c * c_len:(c + 1) * c_len]
    expand = lambda t: _expand_heads(t, SSD_HEADS, SSD_HEADDIM)
    xs = [rows(xbc, c)[:, :WIDTH] for c in chunks]
    bm = [rows(xbc, c)[:, WIDTH:WIDTH + n_st] for c in chunks]
    cm = [rows(xbc, c)[:, WIDTH + n_st:] for c in chunks]
    cum = [_sel_dot(ltri, rows(da, c)) for c in chunks]
    cum_t = [t.T for t in cum]
    dt_t = [rows(dt, c).T for c in chunks]
    yield
    cb = [(_dot_nt(jnp.where(low, cm[c], 0.0), bm[c]), _dot_nt(jnp.where(low, 0.0, cm[c]), bm[c]))
          for c in chunks]
    yield
    upd = [_dot_tn(bm[c], xs[c] * expand(jnp.exp(cum[c][c_len - 1:c_len] - cum[c]) * rows(dt, c)))
           for c in chunks]
    yield
    intra = []
    for c in chunks:
        pairs = []
        for hp in range(SSD_HEADS // 2):
            ws = []
            for hd in (2 * hp, 2 * hp + 1):
                seg = cum[c][:, hd:hd + 1] - cum_t[c][hd:hd + 1, :]
                m = jnp.exp(jnp.where(tril, seg, -jnp.inf)) * dt_t[c][hd:hd + 1, :]
                ws.append((cb[c][hp // 2] * m).astype(BF16))
            xp = xs[c][:, hp * LANE:(hp + 1) * LANE]
            rhs = jnp.concatenate([jnp.where(low, xp, 0.0), jnp.where(low, 0.0, xp)], axis=0)
            pairs.append(_dot(jnp.concatenate(ws, axis=1), rhs))
        intra.append(jnp.concatenate(pairs, axis=1))
        yield
    sts = [st_ref[...]]
    for c in chunks:
        sts.append(sts[c] * expand(jnp.exp(cum[c][c_len - 1:c_len])) + jnp.where(st_mask, upd[c], 0.0))
    st_ref[...] = sts[-1]
    yield
    ys = [intra[c] + _dot(cm[c], sts[c]) * expand(jnp.exp(cum[c])) + dsk_ref[...] * xs[c] for c in chunks]
    yield
    y = jnp.concatenate(ys, axis=0) * _silu(z)
    half = WIDTH // 2
    y = jnp.concatenate([_rmsn(y[:, :half]), _rmsn(y[:, half:])], axis=1)
    o_ref[0] = (y * ng_ref[...]).astype(o_ref.dtype)


def _gla_body(h, wq_ref, wk_ref, wv_ref, wr_ref, wlo_ref, aw_ref, ab_ref, ng_ref,
              o_ref, st_ref, *, tt):
    c_len = GLA_CHUNK
    qk = GLA_HEADS * GLA_DK
    q = jnp.dot(h, wq_ref[...], preferred_element_type=F32) * (GLA_DK ** -0.5)
    k = jnp.dot(h, wk_ref[...], preferred_element_type=F32)
    yield
    v = jnp.dot(h, wv_ref[...], preferred_element_type=F32)
    yield
    r = jnp.dot(h, wr_ref[...], preferred_element_type=F32)
    yield
    lo = jnp.dot(h, wlo_ref[...], preferred_element_type=F32)
    log_a = jax.nn.log_sigmoid(_dot(lo, aw_ref[...]) + ab_ref[...]) / GLA_TAU
    yield

    ltri = _tril(c_len).astype(F32)
    tril_stack = (_iota((GLA_HEADS * c_len, c_len), 0) & (c_len - 1)) >= _iota((GLA_HEADS * c_len, c_len), 1)
    head_of_lane = _iota((c_len, qk), 1) >> 6
    head_of_lane_dv = _iota((GLA_DV, qk), 1) >> 6
    chunks = range(tt // c_len)
    heads = range(GLA_HEADS)
    rows = lambda t, c: t[c * c_len:(c + 1) * c_len]
    stack = lambda t: jnp.concatenate([jnp.where(head_of_lane == hd, t, 0.0) for hd in heads], axis=0)
    b = [_sel_dot(ltri, rows(log_a, c)) for c in chunks]
    ref = [t[c_len // 2:c_len // 2 + 1] for t in b]
    b_last = [t[c_len - 1:c_len] for t in b]
    yield
    att = [jnp.where(tril_stack,
                     _dot_nt(stack(rows(q, c) * jnp.exp(b[c] - ref[c])), rows(k, c) * jnp.exp(ref[c] - b[c])),
                     0.0) for c in chunks]
    yield
    full = [_dot_tn(rows(v, c), rows(k, c) * jnp.exp(b_last[c] - b[c])) for c in chunks]
    yield
    sts = [st_ref[...]]
    for c in chunks:
        upd = sum(jnp.where(head_of_lane_dv == hd, full[c][hd * GLA_DV:(hd + 1) * GLA_DV], 0.0)
                  for hd in heads)
        sts.append(sts[c] * jnp.exp(b_last[c]) + upd)
    st_ref[...] = sts[-1]
    yield
    inter = [_dot_nt(stack(rows(q, c) * jnp.exp(b[c])), sts[c]) for c in chunks]
    yield
    ys = [jnp.concatenate(
        [_dot(att[c][hd * c_len:(hd + 1) * c_len], rows(v, c)[:, hd * GLA_DV:(hd + 1) * GLA_DV])
         + inter[c][hd * c_len:(hd + 1) * c_len] for hd in heads], axis=1) for c in chunks]
    yield
    y = jnp.concatenate(ys, axis=0)
    o = jnp.concatenate(
        [_rmsn(y[:, hd * GLA_DV:(hd + 1) * GLA_DV]) * ng_ref[...] for hd in range(GLA_HEADS)], axis=1)
    o_ref[0] = (o * _silu(r)).astype(o_ref.dtype)


def _lru_body(h, wx_ref, wg_ref, cw_ref, cb_ref, wa_ref, ba_ref, wi_ref, bi_ref,
              lam_ref, o_ref, tail_ref, carry_ref, *, tt):
    xr = jnp.dot(h, wx_ref[...], preferred_element_type=F32)
    yield
    gate = jnp.dot(h, wg_ref[...], preferred_element_type=F32)
    yield
    xb = _causal_conv(xr, tail_ref[...], cw_ref[...], cb_ref[...])
    tail_ref[...] = xr[tt - SUBLANE:]
    yield
    r = _sigmoid(_dot(xb, wa_ref[...]) + ba_ref[...])
    yield
    i = _sigmoid(_dot(xb, wi_ref[...]) + bi_ref[...])
    yield
    log_a = -LRU_C * r * jax.nn.softplus(-lam_ref[...])
    a = jnp.exp(log_a)
    u = jnp.sqrt(1.0 - jnp.exp(2.0 * log_a)) * (i * xb)
    yield

    n_blk = tt // SUBLANE
    a = a.reshape(n_blk, SUBLANE, WIDTH)
    u = u.reshape(n_blk, SUBLANE, WIDTH)
    rid = _iota(a.shape, 1)
    s = 1
    while s < SUBLANE:
        m = rid >= s
        u = jnp.where(m, a * pltpu.roll(u, s, 1) + u, u)
        a = jnp.where(m, a * pltpu.roll(a, s, 1), a)
        s *= 2
        yield
    carry = carry_ref[...]
    blocks = []
    for blk in range(n_blk):
        hb = u[blk] + a[blk] * carry
        blocks.append(hb)
        carry = jnp.broadcast_to(hb[SUBLANE - 1:], (SUBLANE, WIDTH))
        if blk % 16 == 15:
            yield
    carry_ref[...] = carry
    hs = jnp.concatenate(blocks, axis=0)
    o_ref[0] = (hs * jax.nn.gelu(gate, approximate=True)).astype(o_ref.dtype)


def _rwkv_body(h, consts, vres, o_ref, vfo_ref, scratch, *, tt):
    (wr_ref, wk_ref, wv_ref, wlo_ref, mur_ref, muk_ref, muv_ref, mulo_ref,
     w0_ref, ww2_ref, a0_ref, aw2_ref, gw2_ref, kk_ref, ka_ref, rk_ref, lng_ref, lnb_ref,
     hsum_ref) = consts
    has_vres = vres is not None
    if has_vres:
        v0_ref, vw2_ref, vf_ref = vres
    (tr_ref, tk_ref, tv_ref, tlo_ref, s_ref) = scratch
    c_len = RWKV_CHUNK
    n = RWKV_HEAD

    def shifted(w_ref, tail_ref, mu_ref):
        p = jnp.dot(h, w_ref[...], preferred_element_type=F32)
        prev = _shift_rows(p, tail_ref[...], 1)
        tail_ref[...] = p[tt - SUBLANE:]
        return p + (prev - p) * mu_ref[...]

    r = shifted(wr_ref, tr_ref, mur_ref)
    yield
    k = shifted(wk_ref, tk_ref, muk_ref)
    yield
    v = shifted(wv_ref, tv_ref, muv_ref)
    yield
    lo = shifted(wlo_ref, tlo_ref, mulo_ref)
    yield
    w_lo, a_lo, g_lo = lo[:, :LANE], lo[:, LANE:2 * LANE], lo[:, 2 * LANE:3 * LANE]
    w_log = -jax.nn.softplus(-(w0_ref[...] + _dot(jnp.tanh(w_lo), ww2_ref[...]))) - 0.5
    log_w = -jnp.exp(w_log)
    a = _sigmoid(a0_ref[...] + _dot(a_lo, aw2_ref[...]))
    g = _dot(_sigmoid(g_lo), gw2_ref[...])
    yield
    if has_vres:
        v = v + (vf_ref[0] - v) * _sigmoid(v0_ref[...] + _dot(lo[:, 3 * LANE:], vw2_ref[...]))
    else:
        vfo_ref[0] = v
    hsum = hsum_ref[...]
    head_sum = lambda t: _dot_sel(t, hsum, parts=2)
    per_head = lambda t: _expand_heads(t, RWKV_HEADS, n)
    kk = k * kk_ref[...]
    kk = kk * per_head(1.0 / jnp.maximum(jnp.sqrt(head_sum(kk * kk)), 1e-12))
    k = k * (1.0 + (a - 1.0) * ka_ref[...])
    bonus = per_head(head_sum(r * k * rk_ref[...])) * v
    yield

    n_chunks = tt // c_len
    n_pairs = WIDTH // LANE
    ltri = _tril(c_len).astype(F32)
    lc = jnp.concatenate(
        [_sel_dot(ltri, log_w[c * c_len:(c + 1) * c_len]) for c in range(n_chunks)], axis=0)
    yield
    e_pos = jnp.exp(lc)
    e_neg = jnp.exp(-lc)
    rt = r * e_pos
    at = -kk * jnp.exp(lc - log_w)
    bt = kk * a * e_neg
    kt = k * e_neg
    yield

    row = _iota((c_len, LANE), 0)
    col = _iota((c_len, LANE), 1) & (n - 1)
    stril = row > col
    eye = (row == col).astype(F32)
    tril2 = _iota((c_len, 2 * LANE), 0) >= (_iota((c_len, 2 * LANE), 1) & (n - 1))
    low = _iota((c_len, LANE), 1) < n

    def bd(t):
        return jnp.concatenate([jnp.where(low, t, 0.0), jnp.where(low, 0.0, t)], axis=0)

    blocks = [(c, q) for c in range(n_chunks) for q in range(n_pairs)]
    blk = lambda t, cq: t[cq[0] * c_len:(cq[0] + 1) * c_len, cq[1] * LANE:(cq[1] + 1) * LANE]

    e_last_t = jnp.concatenate(
        [e_pos[(c + 1) * c_len - 1:(c + 1) * c_len] for c in range(n_chunks)], axis=0).T

    def decay_cols(cq):
        c, q = cq
        col_of = lambda hd: jnp.broadcast_to(e_last_t[hd * n:(hd + 1) * n, c:c + 1], (c_len, LANE))
        return jnp.where(low, col_of(2 * q), col_of(2 * q + 1))

    vbd = [bd(blk(v, cq)) for cq in blocks]
    yield
    m = [_dot_nt(jnp.concatenate([blk(at, cq), blk(rt, cq), eye], axis=0),
                 jnp.concatenate([bd(blk(bt, cq)), bd(blk(kt, cq))], axis=0)) for cq in blocks]
    yield
    p = [jnp.where(stril, mb[:c_len, :LANE], 0.0) for mb in m]
    rbk = [jnp.concatenate([jnp.where(tril2, mb[c_len:2 * c_len], 0.0), mb[2 * c_len:]], axis=0) for mb in m]
    lakv = [_dot(jnp.where(stril, mb[:c_len, LANE:], 0.0), vb) for mb, vb in zip(m, vbd)]
    yield
    inv = [eye + pb for pb in p]
    span = 2
    while span < c_len:
        p = [_dot(pb, bd(pb)) for pb in p]
        yield
        inv = [ib + _dot(ib, bd(pb)) for ib, pb in zip(inv, p)]
        yield
        span *= 2
    tz = [_dot(ib, jnp.concatenate([bd(blk(at, cq)), bd(lb)], axis=1))
          for ib, lb, cq in zip(inv, lakv, blocks)]
    ar = [jnp.concatenate([tzb[:, :LANE], blk(rt, cq)], axis=0) for tzb, cq in zip(tz, blocks)]
    yield
    e_cols = [decay_cols(cq) for cq in blocks]
    yield

    st = [s_ref[:, q * LANE:(q + 1) * LANE] for q in range(n_pairs)]
    y_rows = []
    for c in range(n_chunks):
        ids = range(c * n_pairs, (c + 1) * n_pairs)
        gs = [_dot(ar[i], bd(st[q])) for q, i in enumerate(ids)]
        yield
        uv = [jnp.concatenate([bd(gs[q][:c_len] + tz[i][:, LANE:]), vbd[i]], axis=0) for q, i in enumerate(ids)]
        yd = [_dot(rbk[i], uv[q]) for q, i in enumerate(ids)]
        y_rows.append(jnp.concatenate([gs[q][c_len:] + yd[q][:c_len] for q in range(n_pairs)], axis=1))
        st = [(st[q] + yd[q][c_len:]) * e_cols[i] for q, i in enumerate(ids)]
        yield
    s_ref[...] = jnp.concatenate(st, axis=1)

    y = jnp.concatenate(y_rows, axis=0)
    mean = per_head(head_sum(y) * (1.0 / n))
    yield
    d = y - mean
    inv_std = per_head(lax.rsqrt(head_sum(d * d) * (1.0 / n) + RWKV_GN_EPS))
    yield
    y = d * inv_std * lng_ref[...] + lnb_ref[...] + bonus
    o_ref[0] = (y * g).astype(o_ref.dtype)


N_SSD_CONSTS, N_GLA_CONSTS, N_LRU_CONSTS, N_RWKV_CONSTS = 9, 8, 9, 19
N_SSD_SCRATCH, N_GLA_SCRATCH, N_LRU_SCRATCH, N_RWKV_SCRATCH = 2, 1, 2, 5
MIX_STEPS = (4, 1, 1, 1)
MIX_DELAY = (0, 0, 0, 6)


def _interleave(gens):
    live = list(gens)
    rnd = 0
    while live:
        nxt = []
        for gen, steps, delay in live:
            if rnd < delay:
                nxt.append((gen, steps, delay))
                continue
            for _ in range(steps):
                if next(gen, StopIteration) is StopIteration:
                    break
            else:
                nxt.append((gen, steps, delay))
        live = nxt
        rnd += 1


def _mixers_kernel(*refs, tt, has_vres):
    refs = list(refs)
    take = lambda k: [refs.pop(0) for _ in range(k)]
    x_ref, g_ref = take(2)
    ssd_c, gla_c, lru_c, rwkv_c = (take(k) for k in (N_SSD_CONSTS, N_GLA_CONSTS, N_LRU_CONSTS, N_RWKV_CONSTS))
    vres = take(3) if has_vres else None
    y_ssd, y_gla, y_lru, y_rwkv = take(4)
    vfo_ref = None if has_vres else take(1)[0]
    ssd_s, gla_s, lru_s, rwkv_s = (take(k) for k in (N_SSD_SCRATCH, N_GLA_SCRATCH, N_LRU_SCRATCH, N_RWKV_SCRATCH))

    @pl.when(pl.program_id(1) == 0)
    def _():
        for ref in ssd_s + gla_s + lru_s + rwkv_s:
            ref[...] = jnp.zeros_like(ref)

    h = _normed_input(x_ref, g_ref)
    _interleave([(_rwkv_body(h, rwkv_c, vres, y_rwkv, vfo_ref, rwkv_s, tt=tt), MIX_STEPS[0], MIX_DELAY[0]),
                 (_ssd_body(h, *ssd_c, y_ssd, *ssd_s, tt=tt), MIX_STEPS[1], MIX_DELAY[1]),
                 (_gla_body(h, *gla_c, y_gla, *gla_s, tt=tt), MIX_STEPS[2], MIX_DELAY[2]),
                 (_lru_body(h, *lru_c, y_lru, *lru_s, tt=tt), MIX_STEPS[3], MIX_DELAY[3])])


def _sub_tiles(tm):
    step = tm // SUB_TILES
    return [slice(s * step, (s + 1) * step) for s in range(SUB_TILES)]


def _merge_rows(rows, x_ref, y_refs, g0_ref, g1_ref, wgate_ref, wbr_ref, wout_ref, o_ref):
    x = x_ref[rows, :]
    h = (_rmsn(x) * g0_ref[...]).astype(BF16)
    yield
    acc = jnp.zeros(x.shape, F32)
    for kb, y_ref in enumerate(y_refs):
        logits = jnp.dot(h, wgate_ref[:, kb * D_MODEL:(kb + 1) * D_MODEL], preferred_element_type=F32)
        branch = jnp.dot(y_ref[rows, :], wbr_ref[kb], preferred_element_type=F32)
        yield
        acc = acc + _sigmoid(logits) * branch
    o = jnp.dot(acc.astype(BF16), wout_ref[...], preferred_element_type=F32)
    yield
    o_ref[rows, :] = x + _rmsn(o) * g1_ref[...]


def _merge_kernel(x_ref, y0_ref, y1_ref, y2_ref, y3_ref, g0_ref, g1_ref, wgate_ref, wbr_ref, wout_ref,
                  o_ref):
    _interleave([(_merge_rows(rows, x_ref, (y0_ref, y1_ref, y2_ref, y3_ref), g0_ref, g1_ref,
                              wgate_ref, wbr_ref, wout_ref, o_ref), 1, s)
                 for s, rows in enumerate(_sub_tiles(x_ref.shape[0]))])


def _ffn_rows(rows, x_ref, g2_ref, g3_ref, wg_ref, wu_ref, wd_ref, o_ref):
    x = x_ref[rows, :]
    h = (_rmsn(x) * g2_ref[...]).astype(BF16)
    yield
    f = jnp.zeros(x.shape, F32)
    step = FFN_HIDDEN // FFN_SPLIT
    for c in range(FFN_SPLIT):
        cols = slice(c * step, (c + 1) * step)
        gt = jnp.dot(h, wg_ref[:, cols], preferred_element_type=F32)
        up = jnp.dot(h, wu_ref[:, cols], preferred_element_type=F32)
        yield
        f = f + jnp.dot((_silu(gt) * up).astype(BF16), wd_ref[cols, :], preferred_element_type=F32)
        yield
    o_ref[rows, :] = x + _rmsn(f) * g3_ref[...]


def _ffn_kernel(x_ref, g2_ref, g3_ref, wg_ref, wu_ref, wd_ref, o_ref):
    _interleave([(_ffn_rows(rows, x_ref, g2_ref, g3_ref, wg_ref, wu_ref, wd_ref, o_ref), 1, s)
                 for s, rows in enumerate(_sub_tiles(x_ref.shape[0]))])


def _const_spec(a):
    nd = a.ndim
    return pl.BlockSpec(a.shape, lambda *_: (0,) * nd, pipeline_mode=pl.Buffered(1))


def _time_tile(seq):
    return min(TIME_TILE, seq)


def _mixers_call(name, x, consts, v_first, scratch):
    bsz, seq, _ = x.shape
    tt = _time_tile(seq)
    assert seq % tt == 0 and tt % SSD_CHUNK == 0
    has_vres = v_first is not None
    tile = lambda width: pl.BlockSpec((1, tt, width), lambda b, t: (b, t, 0))
    tiles_in = [x] + ([v_first] if has_vres else [])
    in_specs = [tile(D_MODEL)] + [_const_spec(c) for c in consts] + ([tile(WIDTH)] if has_vres else [])
    out_shape = [jax.ShapeDtypeStruct((bsz, seq, WIDTH), BF16)] * 4
    if not has_vres:
        out_shape.append(jax.ShapeDtypeStruct((bsz, seq, WIDTH), F32))
    return pl.pallas_call(
        functools.partial(_mixers_kernel, tt=tt, has_vres=has_vres),
        grid=(bsz, seq // tt),
        in_specs=in_specs,
        out_specs=[tile(WIDTH)] * len(out_shape),
        out_shape=out_shape,
        scratch_shapes=list(scratch),
        compiler_params=pltpu.CompilerParams(
            dimension_semantics=("parallel", "arbitrary"), vmem_limit_bytes=VMEM_LIMIT_BYTES),
        name=name,
    )(tiles_in[0], *consts, *tiles_in[1:])


def _token_call(body, name, x2d, tiles, consts):
    tokens = x2d.shape[0]
    tm = min(TOKEN_TILE, tokens)
    assert tokens % tm == 0
    tile = lambda width: pl.BlockSpec((tm, width), lambda i: (i, 0))
    return pl.pallas_call(
        body,
        grid=(tokens // tm,),
        in_specs=[tile(D_MODEL)] + [tile(t.shape[-1]) for t in tiles] + [_const_spec(c) for c in consts],
        out_specs=tile(D_MODEL),
        out_shape=jax.ShapeDtypeStruct(x2d.shape, F32),
        compiler_params=pltpu.CompilerParams(
            dimension_semantics=("parallel",), vmem_limit_bytes=VMEM_LIMIT_BYTES),
        name=name,
    )(x2d, *tiles, *consts)


def _row(v, width=None):
    v = v.reshape(1, -1).astype(F32)
    if width is not None and v.shape[1] < width:
        v = jnp.pad(v, ((0, 0), (0, width - v.shape[1])))
    return v


def _pad_to(w, rows=None, cols=None):
    rows = w.shape[0] if rows is None else rows
    cols = w.shape[1] if cols is None else cols
    return jnp.pad(w, ((0, rows - w.shape[0]), (0, cols - w.shape[1])))


def _split_cols(w, sizes):
    out, start = [], 0
    for nsz in sizes:
        out.append(w[:, start:start + nsz])
        start += nsz
    return out


def _block_diag(w):
    nb, bi, bj = w.shape
    eye = jnp.eye(nb, dtype=w.dtype)
    return jnp.einsum('hij,hg->higj', w, eye).reshape(nb * bi, nb * bj)


def _vmem(*shape):
    return pltpu.VMEM(shape, F32)


def _ssd_operands(p, layer, w_z, w_xbc, w_dt):
    consts = [w_z.astype(BF16), w_xbc.astype(BF16), _pad_to(w_dt, cols=LANE).astype(BF16),
              p['ssd_conv_w'][layer], _row(p['ssd_conv_b'][layer]), _row(p['ssd_dt_bias'][layer], LANE),
              _row(p['ssd_a_log'][layer], LANE), _row(jnp.repeat(p['ssd_d'][layer], SSD_HEADDIM)),
              _row(p['ssd_norm_g'][layer])]
    return consts, [_vmem(SUBLANE, SSD_XBC), _vmem(2 * SSD_STATE, WIDTH)]


def _gla_operands(p, layer, w_q, w_k, w_v, w_r, w_lo):
    consts = [w_q.astype(BF16), w_k.astype(BF16), w_v.astype(BF16), w_r.astype(BF16),
              _pad_to(w_lo, cols=LANE).astype(BF16), _pad_to(p['gla_alpha_w'][layer], rows=LANE).astype(BF16),
              _row(p['gla_alpha_b'][layer]), _row(p['gla_norm_g'][layer])]
    return consts, [_vmem(GLA_DV, GLA_HEADS * GLA_DK)]


def _lru_operands(p, layer, w_x, w_gate):
    consts = [w_x.astype(BF16), w_gate.astype(BF16), p['lru_conv_w'][layer], _row(p['lru_conv_b'][layer]),
              _block_diag(p['lru_wa'][layer]).astype(BF16), _row(p['lru_ba'][layer]),
              _block_diag(p['lru_wx'][layer]).astype(BF16), _row(p['lru_bx'][layer]),
              _row(p['lru_lambda'][layer])]
    return consts, [_vmem(SUBLANE, WIDTH)] * 2


def _rwkv_operands(p, layer, w_rw, has_vres):
    bf = lambda w: w.astype(BF16)
    sizes = (WIDTH, WIDTH, WIDTH) + RWKV_LORAS
    w_r, w_k, w_v, w_wlo, w_alo, w_glo = _split_cols(w_rw, sizes)
    mu_r, mu_k, mu_v, mu_wlo, mu_alo, mu_glo = _split_cols(p['rwkv_mu'][layer].reshape(1, -1), sizes)
    w_vlo = p['w_vres_in'][layer - 1] if has_vres else jnp.zeros((D_MODEL, RWKV_V_LORA), F32)
    lo_w = jnp.concatenate([_pad_to(w, cols=LANE) for w in (w_wlo, w_alo, w_glo, w_vlo)], axis=1)
    lo_mu = jnp.concatenate([_pad_to(m, cols=LANE) for m in (mu_wlo, mu_alo, mu_glo)]
                            + [jnp.zeros((1, LANE), F32)], axis=1)
    head_sum = (jnp.arange(WIDTH)[:, None] // RWKV_HEAD == jnp.arange(LANE)[None, :])
    consts = [bf(w_r), bf(w_k), bf(w_v), bf(lo_w), mu_r, mu_k, mu_v, lo_mu,
              _row(p['rwkv_w0'][layer]), bf(_pad_to(p['rwkv_w_w2'][layer], rows=LANE)),
              _row(p['rwkv_a0'][layer]), bf(_pad_to(p['rwkv_a_w2'][layer], rows=LANE)),
              bf(_pad_to(p['rwkv_g_w2'][layer], rows=LANE)),
              _row(p['rwkv_k_k'][layer]), _row(p['rwkv_k_a'][layer]), _row(p['rwkv_r_k'][layer]),
              _row(p['rwkv_ln_g'][layer]), _row(p['rwkv_ln_b'][layer]), bf(head_sum)]
    if has_vres:
        consts += [_row(p['rwkv_v0'][layer - 1]), bf(_pad_to(p['rwkv_v_w2'][layer - 1], rows=LANE))]
    return consts, [_vmem(SUBLANE, WIDTH)] * 4 + [_vmem(RWKV_HEAD, WIDTH)]


def _mixers(x, p, layer, g0, w_cols, v_first):
    (w_sz, w_sxbc, w_sdt, w_gq, w_gk, w_gv, w_gr, w_glo, w_lx, w_lg, w_rw) = w_cols
    parts = [_ssd_operands(p, layer, w_sz, w_sxbc, w_sdt),
             _gla_operands(p, layer, w_gq, w_gk, w_gv, w_gr, w_glo),
             _lru_operands(p, layer, w_lx, w_lg),
             _rwkv_operands(p, layer, w_rw, v_first is not None)]
    consts = [g0] + [c for cs, _ in parts for c in cs]
    scratch = [s for _, ss in parts for s in ss]
    outs = _mixers_call(f"mixers_{layer}", x, consts, v_first, scratch)
    return list(outs[:4]), (v_first if v_first is not None else outs[4])


def _merge(x2d, ys, p, layer, g0, w_gates):
    return _token_call(
        _merge_kernel, f"merge_{layer}", x2d, ys,
        [g0, _row(p['norm_g'][layer, 1]), w_gates.astype(BF16), p['w_branch'][layer].astype(BF16),
         p['w_out'][layer].astype(BF16)])


def _ffn(x2d, p, layer):
    return _token_call(
        _ffn_kernel, f"ffn_{layer}", x2d, [],
        [_row(p['norm_g'][layer, 2]), _row(p['norm_g'][layer, 3]), p['ffn_w_gate'][layer].astype(BF16),
         p['ffn_w_up'][layer].astype(BF16), p['ffn_w_down'][layer].astype(BF16)])


def kernel(x, norm_g, w_in, w_vres_in, ssd_conv_w, ssd_conv_b, ssd_dt_bias, ssd_a_log, ssd_d, ssd_norm_g,
           gla_alpha_w, gla_alpha_b, gla_norm_g, lru_conv_w, lru_conv_b, lru_wa, lru_ba, lru_wx, lru_bx,
           lru_lambda, rwkv_mu, rwkv_w0, rwkv_w_w2, rwkv_a0, rwkv_a_w2, rwkv_v0, rwkv_v_w2, rwkv_g_w2,
           rwkv_k_k, rwkv_k_a, rwkv_r_k, rwkv_ln_g, rwkv_ln_b, w_branch, w_out,
           ffn_w_gate, ffn_w_up, ffn_w_down):
    p = dict(locals())
    bsz, seq, _ = x.shape
    flat = lambda t: t.reshape(bsz * seq, t.shape[-1])
    v_first = None
    for layer in range(w_in.shape[0]):
        g0 = _row(norm_g[layer, 0])
        *w_cols, w_gates = _split_cols(w_in[layer], IN_SIZES)
        ys, v_first = _mixers(x, p, layer, g0, w_cols, v_first)
        x2d = _merge(flat(x), [flat(y) for y in ys], p, layer, g0, w_gates)
        x = _ffn(x2d, p, layer).reshape(bsz, seq, D_MODEL)
    return x
```

```python
import functools

import jax
import jax.numpy as jnp
from jax import lax
from jax.experimental import pallas as pl
from jax.experimental.pallas import tpu as pltpu

F32 = jnp.float32
BF16 = jnp.bfloat16

D_MODEL = 1024
WIDTH = 512
RMS_EPS = 1e-6
CONV_WIDTH = 4
SSD_HEADS = 8
SSD_HEADDIM = 64
SSD_STATE = 64
SSD_CHUNK = 128
SSD_XBC = WIDTH + 4 * SSD_STATE
GLA_HEADS = 4
GLA_DK = 64
GLA_DV = 128
GLA_LOWRANK = 16
GLA_TAU = 16.0
GLA_CHUNK = 64
LRU_C = 8.0
RWKV_HEADS = 8
RWKV_HEAD = 64
RWKV_CHUNK = 64
RWKV_GN_EPS = 64e-5
RWKV_LORAS = (32, 32, 96)
RWKV_V_LORA = 32
FFN_HIDDEN = 2816
FFN_SPLIT = 2
LANE = 128
SUBLANE = 8
TIME_TILE = 512
TOKEN_TILE = 1024
SUB_TILES = 2
VMEM_LIMIT_BYTES = 56 * 1024 * 1024

IN_SIZES = (WIDTH, SSD_XBC, SSD_HEADS,
            GLA_HEADS * GLA_DK, GLA_HEADS * GLA_DK, WIDTH, WIDTH, GLA_LOWRANK,
            WIDTH, WIDTH,
            3 * WIDTH + sum(RWKV_LORAS),
            4 * D_MODEL)
N_IN = sum(IN_SIZES)


def _rmsn(x):
    return x * lax.rsqrt(jnp.mean(x * x, axis=-1, keepdims=True) + RMS_EPS)


def _dot(a, b):
    return jnp.dot(a.astype(BF16), b.astype(BF16), preferred_element_type=F32)


def _dot_nt(a, b):
    return lax.dot_general(a.astype(BF16), b.astype(BF16), (((1,), (1,)), ((), ())),
                           preferred_element_type=F32)


def _dot_tn(a, b):
    return lax.dot_general(a.astype(BF16), b.astype(BF16), (((0,), (0,)), ((), ())),
                           preferred_element_type=F32)


def _split(x, parts):
    out = []
    for _ in range(parts - 1):
        hi = x.astype(BF16)
        out.append(hi)
        x = x - hi.astype(F32)
    return out + [x.astype(BF16)]


def _sel_dot(sel, x, parts=3):
    s = sel.astype(BF16)
    return sum(jnp.dot(s, p, preferred_element_type=F32) for p in _split(x, parts))


def _dot_sel(x, sel, parts=3):
    s = sel.astype(BF16)
    return sum(jnp.dot(p, s, preferred_element_type=F32) for p in _split(x, parts))


def _iota(shape, axis):
    return lax.broadcasted_iota(jnp.int32, shape, axis)


def _tril(n, m=None, strict=False):
    m = n if m is None else m
    r, c = _iota((n, m), 0), _iota((n, m), 1)
    return (r > c) if strict else (r >= c)


def _shift_rows(x, tail, j):
    tt, width = x.shape
    r3 = pltpu.roll(x.reshape(tt // SUBLANE, SUBLANE, width), j, 1)
    prev = jnp.concatenate([pltpu.roll(tail, j, 0)[None], r3[:-1]], axis=0)
    return jnp.where(_iota(r3.shape, 1) < j, prev, r3).reshape(tt, width)


def _causal_conv(x, tail, w, b):
    y = x * w[CONV_WIDTH - 1:CONV_WIDTH] + b
    for j in range(1, CONV_WIDTH):
        y = y + _shift_rows(x, tail, j) * w[CONV_WIDTH - 1 - j:CONV_WIDTH - j]
    return y


def _expand_heads(v, n_heads, width):
    rows = v.shape[0]
    return jnp.concatenate(
        [jnp.broadcast_to(v[:, h:h + 1], (rows, width)) for h in range(n_heads)], axis=1)


def _sigmoid(x):
    return jax.nn.sigmoid(x)


def _silu(x):
    return x * jax.nn.sigmoid(x)


def _normed_input(x_ref, g_ref):
    return (_rmsn(x_ref[0]) * g_ref[...]).astype(BF16)


def _ssd_body(h, wz_ref, wxbc_ref, wdt_ref, cw_ref, cb_ref, dtb_ref, alog_ref,
              dsk_ref, ng_ref, o_ref, tail_ref, st_ref, *, tt):
    c_len = SSD_CHUNK
    n_st = 2 * SSD_STATE
    z = jnp.dot(h, wz_ref[...], preferred_element_type=F32)
    yield
    xbc_raw = jnp.dot(h, wxbc_ref[...], preferred_element_type=F32)
    dt_raw = jnp.dot(h, wdt_ref[...], preferred_element_type=F32)
    yield
    xbc = _silu(_causal_conv(xbc_raw, tail_ref[...], cw_ref[...], cb_ref[...]))
    tail_ref[...] = xbc_raw[tt - SUBLANE:]
    dt = jax.nn.softplus(dt_raw + dtb_ref[...])
    da = dt * (-jnp.exp(alog_ref[...]))
    yield

    ltri = _tril(c_len).astype(F32)
    tril = _tril(c_len)
    lane = _iota((c_len, LANE), 1)
    low = lane < SSD_HEADDIM
    st_mask = (_iota((n_st, WIDTH), 0) < SSD_STATE) == (_iota((n_st, WIDTH), 1) < WIDTH // 2)
    chunks = range(tt // c_len)
    rows = lambda t, c: t[c * c_len:(c + 1) * c_len]
    expand = lambda t: _expand_heads(t, SSD_HEADS, SSD_HEADDIM)
    xs = [rows(xbc, c)[:, :WIDTH] for c in chunks]
    bm = [rows(xbc, c)[:, WIDTH:WIDTH + n_st] for c in chunks]
    cm = [rows(xbc, c)[:, WIDTH + n_st:] for c in chunks]
    cum = [_sel_dot(ltri, rows(da, c)) for c in chunks]
    cum_t = [t.T for t in cum]
    dt_t = [rows(dt, c).T for c in chunks]
    yield
    cb = [(_dot_nt(jnp.where(low, cm[c], 0.0), bm[c]), _dot_nt(jnp.where(low, 0.0, cm[c]), bm[c]))
          for c in chunks]
    yield
    upd = [_dot_tn(bm[c], xs[c] * expand(jnp.exp(cum[c][c_len - 1:c_len] - cum[c]) * rows(dt, c)))
           for c in chunks]
    yield
    intra = []
    for c in chunks:
        pairs = []
        for hp in range(SSD_HEADS // 2):
            ws = []
            for hd in (2 * hp, 2 * hp + 1):
                seg = cum[c][:, hd:hd + 1] - cum_t[c][hd:hd + 1, :]
                m = jnp.exp(jnp.where(tril, seg, -jnp.inf)) * dt_t[c][hd:hd + 1, :]
                ws.append((cb[c][hp // 2] * m).astype(BF16))
            xp = xs[c][:, hp * LANE:(hp + 1) * LANE]
            rhs = jnp.concatenate([jnp.where(low, xp, 0.0), jnp.where(low, 0.0, xp)], axis=0)
            pairs.append(_dot(jnp.concatenate(ws, axis=1), rhs))
        intra.append(jnp.concatenate(pairs, axis=1))
        yield
    sts = [st_ref[...]]
    for c in chunks:
        sts.append(sts[c] * expand(jnp.exp(cum[c][c_len - 1:c_len])) + jnp.where(st_mask, upd[c], 0.0))
    st_ref[...] = sts[-1]
    yield
    ys = [intra[c] + _dot(cm[c], sts[c]) * expand(jnp.exp(cum[c])) + dsk_ref[...] * xs[c] for c in chunks]
    yield
    y = jnp.concatenate(ys, axis=0) * _silu(z)
    half = WIDTH // 2
    y = jnp.concatenate([_rmsn(y[:, :half]), _rmsn(y[:, half:])], axis=1)
    o_ref[0] = (y * ng_ref[...]).astype(o_ref.dtype)


def _gla_body(h, wq_ref, wk_ref, wv_ref, wr_ref, wlo_ref, aw_ref, ab_ref, ng_ref,
              o_ref, st_ref, *, tt):
    c_len = GLA_CHUNK
    qk = GLA_HEADS * GLA_DK
    q = jnp.dot(h, wq_ref[...], preferred_element_type=F32) * (GLA_DK ** -0.5)
    k = jnp.dot(h, wk_ref[...], preferred_element_type=F32)
    yield
    v = jnp.dot(h, wv_ref[...], preferred_element_type=F32)
    yield
    r = jnp.dot(h, wr_ref[...], preferred_element_type=F32)
    yield
    lo = jnp.dot(h, wlo_ref[...], preferred_element_type=F32)
    log_a = jax.nn.log_sigmoid(_dot(lo, aw_ref[...]) + ab_ref[...]) / GLA_TAU
    yield

    ltri = _tril(c_len).astype(F32)
    tril_stack = (_iota((GLA_HEADS * c_len, c_len), 0) & (c_len - 1)) >= _iota((GLA_HEADS * c_len, c_len), 1)
    head_of_lane = _iota((c_len, qk), 1) >> 6
    head_of_lane_dv = _iota((GLA_DV, qk), 1) >> 6
    chunks = range(tt // c_len)
    heads = range(GLA_HEADS)
    rows = lambda t, c: t[c * c_len:(c + 1) * c_len]
    stack = lambda t: jnp.concatenate([jnp.where(head_of_lane == hd, t, 0.0) for hd in heads], axis=0)
    b = [_sel_dot(ltri, rows(log_a, c)) for c in chunks]
    ref = [t[c_len // 2:c_len // 2 + 1] for t in b]
    b_last = [t[c_len - 1:c_len] for t in b]
    yield
    att = [jnp.where(tril_stack,
                     _dot_nt(stack(rows(q, c) * jnp.exp(b[c] - ref[c])), rows(k, c) * jnp.exp(ref[c] - b[c])),
                     0.0) for c in chunks]
    yield
    full = [_dot_tn(rows(v, c), rows(k, c) * jnp.exp(b_last[c] - b[c])) for c in chunks]
    yield
    sts = [st_ref[...]]
    for c in chunks:
        upd = sum(jnp.where(head_of_lane_dv == hd, full[c][hd * GLA_DV:(hd + 1) * GLA_DV], 0.0)
                  for hd in heads)
        sts.append(sts[c] * jnp.exp(b_last[c]) + upd)
    st_ref[...] = sts[-1]
    yield
    inter = [_dot_nt(stack(rows(q, c) * jnp.exp(b[c])), sts[c]) for c in chunks]
    yield
    ys = [jnp.concatenate(
        [_dot(att[c][hd * c_len:(hd + 1) * c_len], rows(v, c)[:, hd * GLA_DV:(hd + 1) * GLA_DV])
         + inter[c][hd * c_len:(hd + 1) * c_len] for hd in heads], axis=1) for c in chunks]
    yield
    y = jnp.concatenate(ys, axis=0)
    o = jnp.concatenate(
        [_rmsn(y[:, hd * GLA_DV:(hd + 1) * GLA_DV]) * ng_ref[...] for hd in range(GLA_HEADS)], axis=1)
    o_ref[0] = (o * _silu(r)).astype(o_ref.dtype)


def _lru_body(h, wx_ref, wg_ref, cw_ref, cb_ref, wa_ref, ba_ref, wi_ref, bi_ref,
              lam_ref, o_ref, tail_ref, carry_ref, *, tt):
    xr = jnp.dot(h, wx_ref[...], preferred_element_type=F32)
    yield
    gate = jnp.dot(h, wg_ref[...], preferred_element_type=F32)
    yield
    xb = _causal_conv(xr, tail_ref[...], cw_ref[...], cb_ref[...])
    tail_ref[...] = xr[tt - SUBLANE:]
    yield
    r = _sigmoid(_dot(xb, wa_ref[...]) + ba_ref[...])
    yield
    i = _sigmoid(_dot(xb, wi_ref[...]) + bi_ref[...])
    yield
    log_a = -LRU_C * r * jax.nn.softplus(-lam_ref[...])
    a = jnp.exp(log_a)
    u = jnp.sqrt(1.0 - jnp.exp(2.0 * log_a)) * (i * xb)
    yield

    n_blk = tt // SUBLANE
    a = a.reshape(n_blk, SUBLANE, WIDTH)
    u = u.reshape(n_blk, SUBLANE, WIDTH)
    rid = _iota(a.shape, 1)
    s = 1
    while s < SUBLANE:
        m = rid >= s
        u = jnp.where(m, a * pltpu.roll(u, s, 1) + u, u)
        a = jnp.where(m, a * pltpu.roll(a, s, 1), a)
        s *= 2
        yield
    carry = carry_ref[...]
    blocks = []
    for blk in range(n_blk):
        hb = u[blk] + a[blk] * carry
        blocks.append(hb)
        carry = jnp.broadcast_to(hb[SUBLANE - 1:], (SUBLANE, WIDTH))
        if blk % 16 == 15:
            yield
    carry_ref[...] = carry
    hs = jnp.concatenate(blocks, axis=0)
    o_ref[0] = (hs * jax.nn.gelu(gate, approximate=True)).astype(o_ref.dtype)


def _rwkv_body(h, consts, vres, o_ref, vfo_ref, scratch, *, tt):
    (wr_ref, wk_ref, wv_ref, wlo_ref, mur_ref, muk_ref, muv_ref, mulo_ref,
     w0_ref, ww2_ref, a0_ref, aw2_ref, gw2_ref, kk_ref, ka_ref, rk_ref, lng_ref, lnb_ref,
     hsum_ref) = consts
    has_vres = vres is not None
    if has_vres:
        v0_ref, vw2_ref, vf_ref = vres
    (tr_ref, tk_ref, tv_ref, tlo_ref, s_ref) = scratch
    c_len = RWKV_CHUNK
    n = RWKV_HEAD

    def shifted(w_ref, tail_ref, mu_ref):
        p = jnp.dot(h, w_ref[...], preferred_element_type=F32)
        prev = _shift_rows(p, tail_ref[...], 1)
        tail_ref[...] = p[tt - SUBLANE:]
        return p + (prev - p) * mu_ref[...]

    r = shifted(wr_ref, tr_ref, mur_ref)
    yield
    k = shifted(wk_ref, tk_ref, muk_ref)
    yield
    v = shifted(wv_ref, tv_ref, muv_ref)
    yield
    lo = shifted(wlo_ref, tlo_ref, mulo_ref)
    yield
    w_lo, a_lo, g_lo = lo[:, :LANE], lo[:, LANE:2 * LANE], lo[:, 2 * LANE:3 * LANE]
    w_log = -jax.nn.softplus(-(w0_ref[...] + _dot(jnp.tanh(w_lo), ww2_ref[...]))) - 0.5
    log_w = -jnp.exp(w_log)
    a = _sigmoid(a0_ref[...] + _dot(a_lo, aw2_ref[...]))
    g = _dot(_sigmoid(g_lo), gw2_ref[...])
    yield
    if has_vres:
        v = v + (vf_ref[0] - v) * _sigmoid(v0_ref[...] + _dot(lo[:, 3 * LANE:], vw2_ref[...]))
    else:
        vfo_ref[0] = v
    hsum = hsum_ref[...]
    head_sum = lambda t: _dot_sel(t, hsum, parts=2)
    per_head = lambda t: _expand_heads(t, RWKV_HEADS, n)
    kk = k * kk_ref[...]
    kk = kk * per_head(1.0 / jnp.maximum(jnp.sqrt(head_sum(kk * kk)), 1e-12))
    k = k * (1.0 + (a - 1.0) * ka_ref[...])
    bonus = per_head(head_sum(r * k * rk_ref[...])) * v
    yield

    n_chunks = tt // c_len
    n_pairs = WIDTH // LANE
    ltri = _tril(c_len).astype(F32)
    lc = jnp.concatenate(
        [_sel_dot(ltri, log_w[c * c_len:(c + 1) * c_len]) for c in range(n_chunks)], axis=0)
    yield
    e_pos = jnp.exp(lc)
    e_neg = jnp.exp(-lc)
    rt = r * e_pos
    at = -kk * jnp.exp(lc - log_w)
    bt = kk * a * e_neg
    kt = k * e_neg
    yield

    row = _iota((c_len, LANE), 0)
    col = _iota((c_len, LANE), 1) & (n - 1)
    stril = row > col
    eye = (row == col).astype(F32)
    tril2 = _iota((c_len, 2 * LANE), 0) >= (_iota((c_len, 2 * LANE), 1) & (n - 1))
    low = _iota((c_len, LANE), 1) < n

    def bd(t):
        return jnp.concatenate([jnp.where(low, t, 0.0), jnp.where(low, 0.0, t)], axis=0)

    blocks = [(c, q) for c in range(n_chunks) for q in range(n_pairs)]
    blk = lambda t, cq: t[cq[0] * c_len:(cq[0] + 1) * c_len, cq[1] * LANE:(cq[1] + 1) * LANE]

    e_last_t = jnp.concatenate(
        [e_pos[(c + 1) * c_len - 1:(c + 1) * c_len] for c in range(n_chunks)], axis=0).T

    def decay_cols(cq):
        c, q = cq
        col_of = lambda hd: jnp.broadcast_to(e_last_t[hd * n:(hd + 1) * n, c:c + 1], (c_len, LANE))
        return jnp.where(low, col_of(2 * q), col_of(2 * q + 1))

    vbd = [bd(blk(v, cq)) for cq in blocks]
    yield
    m = [_dot_nt(jnp.concatenate([blk(at, cq), blk(rt, cq), eye], axis=0),
                 jnp.concatenate([bd(blk(bt, cq)), bd(blk(kt, cq))], axis=0)) for cq in blocks]
    yield
    p = [jnp.where(stril, mb[:c_len, :LANE], 0.0) for mb in m]
    rbk = [jnp.concatenate([jnp.where(tril2, mb[c_len:2 * c_len], 0.0), mb[2 * c_len:]], axis=0) for mb in m]
    lakv = [_dot(jnp.where(stril, mb[:c_len, LANE:], 0.0), vb) for mb, vb in zip(m, vbd)]
    yield
    inv = [eye + pb for pb in p]
    span = 2
    while span < c_len:
        p = [_dot(pb, bd(pb)) for pb in p]
        yield
        inv = [ib + _dot(ib, bd(pb)) for ib, pb in zip(inv, p)]
        yield
        span *= 2
    tz = [_dot(ib, jnp.concatenate([bd(blk(at, cq)), bd(lb)], axis=1))
          for ib, lb, cq in zip(inv, lakv, blocks)]
    ar = [jnp.concatenate([tzb[:, :LANE], blk(rt, cq)], axis=0) for tzb, cq in zip(tz, blocks)]
    yield
    e_cols = [decay_cols(cq) for cq in blocks]
    yield

    st = [s_ref[:, q * LANE:(q + 1) * LANE] for q in range(n_pairs)]
    y_rows = []
    for c in range(n_chunks):
        ids = range(c * n_pairs, (c + 1) * n_pairs)
        gs = [_dot(ar[i], bd(st[q])) for q, i in enumerate(ids)]
        yield
        uv = [jnp.concatenate([bd(gs[q][:c_len] + tz[i][:, LANE:]), vbd[i]], axis=0) for q, i in enumerate(ids)]
        yd = [_dot(rbk[i], uv[q]) for q, i in enumerate(ids)]
        y_rows.append(jnp.concatenate([gs[q][c_len:] + yd[q][:c_len] for q in range(n_pairs)], axis=1))
        st = [(st[q] + yd[q][c_len:]) * e_cols[i] for q, i in enumerate(ids)]
        yield
    s_ref[...] = jnp.concatenate(st, axis=1)

    y = jnp.concatenate(y_rows, axis=0)
    mean = per_head(head_sum(y) * (1.0 / n))
    yield
    d = y - mean
    inv_std = per_head(lax.rsqrt(head_sum(d * d) * (1.0 / n) + RWKV_GN_EPS))
    yield
    y = d * inv_std * lng_ref[...] + lnb_ref[...] + bonus
    o_ref[0] = (y * g).astype(o_ref.dtype)


N_SSD_CONSTS, N_GLA_CONSTS, N_LRU_CONSTS, N_RWKV_CONSTS = 9, 8, 9, 19
N_SSD_SCRATCH, N_GLA_SCRATCH, N_LRU_SCRATCH, N_RWKV_SCRATCH = 2, 1, 2, 5
MIX_STEPS = (4, 1, 1, 1)
MIX_DELAY = (0, 0, 0, 6)


def _interleave(gens):
    live = list(gens)
    rnd = 0
    while live:
        nxt = []
        for gen, steps, delay in live:
            if rnd < delay:
                nxt.append((gen, steps, delay))
                continue
            for _ in range(steps):
                if next(gen, StopIteration) is StopIteration:
                    break
            else:
                nxt.append((gen, steps, delay))
        live = nxt
        rnd += 1


def _mixers_kernel(*refs, tt, has_vres):
    refs = list(refs)
    take = lambda k: [refs.pop(0) for _ in range(k)]
    x_ref, g_ref = take(2)
    ssd_c, gla_c, lru_c, rwkv_c = (take(k) for k in (N_SSD_CONSTS, N_GLA_CONSTS, N_LRU_CONSTS, N_RWKV_CONSTS))
    vres = take(3) if has_vres else None
    y_ssd, y_gla, y_lru, y_rwkv = take(4)
    vfo_ref = None if has_vres else take(1)[0]
    ssd_s, gla_s, lru_s, rwkv_s = (take(k) for k in (N_SSD_SCRATCH, N_GLA_SCRATCH, N_LRU_SCRATCH, N_RWKV_SCRATCH))

    @pl.when(pl.program_id(1) == 0)
    def _():
        for ref in ssd_s + gla_s + lru_s + rwkv_s:
            ref[...] = jnp.zeros_like(ref)

    h = _normed_input(x_ref, g_ref)
    _interleave([(_rwkv_body(h, rwkv_c, vres, y_rwkv, vfo_ref, rwkv_s, tt=tt), MIX_STEPS[0], MIX_DELAY[0]),
                 (_ssd_body(h, *ssd_c, y_ssd, *ssd_s, tt=tt), MIX_STEPS[1], MIX_DELAY[1]),
                 (_gla_body(h, *gla_c, y_gla, *gla_s, tt=tt), MIX_STEPS[2], MIX_DELAY[2]),
                 (_lru_body(h, *lru_c, y_lru, *lru_s, tt=tt), MIX_STEPS[3], MIX_DELAY[3])])


def _sub_tiles(tm):
    step = tm // SUB_TILES
    return [slice(s * step, (s + 1) * step) for s in range(SUB_TILES)]


def _merge_rows(rows, x_ref, y_refs, g0_ref, g1_ref, wgate_ref, wbr_ref, wout_ref, o_ref):
    x = x_ref[rows, :]
    h = (_rmsn(x) * g0_ref[...]).astype(BF16)
    yield
    acc = jnp.zeros(x.shape, F32)
    for kb, y_ref in enumerate(y_refs):
        logits = jnp.dot(h, wgate_ref[:, kb * D_MODEL:(kb + 1) * D_MODEL], preferred_element_type=F32)
        branch = jnp.dot(y_ref[rows, :], wbr_ref[kb], preferred_element_type=F32)
        yield
        acc = acc + _sigmoid(logits) * branch
    o = jnp.dot(acc.astype(BF16), wout_ref[...], preferred_element_type=F32)
    yield
    o_ref[rows, :] = x + _rmsn(o) * g1_ref[...]


def _merge_kernel(x_ref, y0_ref, y1_ref, y2_ref, y3_ref, g0_ref, g1_ref, wgate_ref, wbr_ref, wout_ref,
                  o_ref):
    _interleave([(_merge_rows(rows, x_ref, (y0_ref, y1_ref, y2_ref, y3_ref), g0_ref, g1_ref,
                              wgate_ref, wbr_ref, wout_ref, o_ref), 1, s)
                 for s, rows in enumerate(_sub_tiles(x_ref.shape[0]))])


def _ffn_rows(rows, x_ref, g2_ref, g3_ref, wg_ref, wu_ref, wd_ref, o_ref):
    x = x_ref[rows, :]
    h = (_rmsn(x) * g2_ref[...]).astype(BF16)
    yield
    f = jnp.zeros(x.shape, F32)
    step = FFN_HIDDEN // FFN_SPLIT
    for c in range(FFN_SPLIT):
        cols = slice(c * step, (c + 1) * step)
        gt = jnp.dot(h, wg_ref[:, cols], preferred_element_type=F32)
        up = jnp.dot(h, wu_ref[:, cols], preferred_element_type=F32)
        yield
        f = f + jnp.dot((_silu(gt) * up).astype(BF16), wd_ref[cols, :], preferred_element_type=F32)
        yield
    o_ref[rows, :] = x + _rmsn(f) * g3_ref[...]


def _ffn_kernel(x_ref, g2_ref, g3_ref, wg_ref, wu_ref, wd_ref, o_ref):
    _interleave([(_ffn_rows(rows, x_ref, g2_ref, g3_ref, wg_ref, wu_ref, wd_ref, o_ref), 1, s)
                 for s, rows in enumerate(_sub_tiles(x_ref.shape[0]))])


def _const_spec(a):
    nd = a.ndim
    return pl.BlockSpec(a.shape, lambda *_: (0,) * nd, pipeline_mode=pl.Buffered(1))


def _time_tile(seq):
    return min(TIME_TILE, seq)


def _mixers_call(name, x, consts, v_first, scratch):
    bsz, seq, _ = x.shape
    tt = _time_tile(seq)
    assert seq % tt == 0 and tt % SSD_CHUNK == 0
    has_vres = v_first is not None
    tile = lambda width: pl.BlockSpec((1, tt, width), lambda b, t: (b, t, 0))
    tiles_in = [x] + ([v_first] if has_vres else [])
    in_specs = [tile(D_MODEL)] + [_const_spec(c) for c in consts] + ([tile(WIDTH)] if has_vres else [])
    out_shape = [jax.ShapeDtypeStruct((bsz, seq, WIDTH), BF16)] * 4
    if not has_vres:
        out_shape.append(jax.ShapeDtypeStruct((bsz, seq, WIDTH), F32))
    return pl.pallas_call(
        functools.partial(_mixers_kernel, tt=tt, has_vres=has_vres),
        grid=(bsz, seq // tt),
        in_specs=in_specs,
        out_specs=[tile(WIDTH)] * len(out_shape),
        out_shape=out_shape,
        scratch_shapes=list(scratch),
        compiler_params=pltpu.CompilerParams(
            dimension_semantics=("parallel", "arbitrary"), vmem_limit_bytes=VMEM_LIMIT_BYTES),
        name=name,
    )(tiles_in[0], *consts, *tiles_in[1:])


def _token_call(body, name, x2d, tiles, consts):
    tokens = x2d.shape[0]
    tm = min(TOKEN_TILE, tokens)
    assert tokens % tm == 0
    tile = lambda width: pl.BlockSpec((tm, width), lambda i: (i, 0))
    return pl.pallas_call(
        body,
        grid=(tokens // tm,),
        in_specs=[tile(D_MODEL)] + [tile(t.shape[-1]) for t in tiles] + [_const_spec(c) for c in consts],
        out_specs=tile(D_MODEL),
        out_shape=jax.ShapeDtypeStruct(x2d.shape, F32),
        compiler_params=pltpu.CompilerParams(
            dimension_semantics=("parallel",), vmem_limit_bytes=VMEM_LIMIT_BYTES),
        name=name,
    )(x2d, *tiles, *consts)


def _row(v, width=None):
    v = v.reshape(1, -1).astype(F32)
    if width is not None and v.shape[1] < width:
        v = jnp.pad(v, ((0, 0), (0, width - v.shape[1])))
    return v


def _pad_to(w, rows=None, cols=None):
    rows = w.shape[0] if rows is None else rows
    cols = w.shape[1] if cols is None else cols
    return jnp.pad(w, ((0, rows - w.shape[0]), (0, cols - w.shape[1])))


def _split_cols(w, sizes):
    out, start = [], 0
    for nsz in sizes:
        out.append(w[:, start:start + nsz])
        start += nsz
    return out


def _block_diag(w):
    nb, bi, bj = w.shape
    eye = jnp.eye(nb, dtype=w.dtype)
    return jnp.einsum('hij,hg->higj', w, eye).reshape(nb * bi, nb * bj)


def _vmem(*shape):
    return pltpu.VMEM(shape, F32)


def _ssd_operands(p, layer, w_z, w_xbc, w_dt):
    consts = [w_z.astype(BF16), w_xbc.astype(BF16), _pad_to(w_dt, cols=LANE).astype(BF16),
              p['ssd_conv_w'][layer], _row(p['ssd_conv_b'][layer]), _row(p['ssd_dt_bias'][layer], LANE),
              _row(p['ssd_a_log'][layer], LANE), _row(jnp.repeat(p['ssd_d'][layer], SSD_HEADDIM)),
              _row(p['ssd_norm_g'][layer])]
    return consts, [_vmem(SUBLANE, SSD_XBC), _vmem(2 * SSD_STATE, WIDTH)]


def _gla_operands(p, layer, w_q, w_k, w_v, w_r, w_lo):
    consts = [w_q.astype(BF16), w_k.astype(BF16), w_v.astype(BF16), w_r.astype(BF16),
              _pad_to(w_lo, cols=LANE).astype(BF16), _pad_to(p['gla_alpha_w'][layer], rows=LANE).astype(BF16),
              _row(p['gla_alpha_b'][layer]), _row(p['gla_norm_g'][layer])]
    return consts, [_vmem(GLA_DV, GLA_HEADS * GLA_DK)]


def _lru_operands(p, layer, w_x, w_gate):
    consts = [w_x.astype(BF16), w_gate.astype(BF16), p['lru_conv_w'][layer], _row(p['lru_conv_b'][layer]),
              _block_diag(p['lru_wa'][layer]).astype(BF16), _row(p['lru_ba'][layer]),
              _block_diag(p['lru_wx'][layer]).astype(BF16), _row(p['lru_bx'][layer]),
              _row(p['lru_lambda'][layer])]
    return consts, [_vmem(SUBLANE, WIDTH)] * 2


def _rwkv_operands(p, layer, w_rw, has_vres):
    bf = lambda w: w.astype(BF16)
    sizes = (WIDTH, WIDTH, WIDTH) + RWKV_LORAS
    w_r, w_k, w_v, w_wlo, w_alo, w_glo = _split_cols(w_rw, sizes)
    mu_r, mu_k, mu_v, mu_wlo, mu_alo, mu_glo = _split_cols(p['rwkv_mu'][layer].reshape(1, -1), sizes)
    w_vlo = p['w_vres_in'][layer - 1] if has_vres else jnp.zeros((D_MODEL, RWKV_V_LORA), F32)
    w_vlo = w_vlo.astype(w_rw.dtype)
    lo_w =jnp.concatenate([_pad_to(w, cols=LANE) for w in (w_wlo, w_alo, w_glo, w_vlo)], axis=1)
    lo_mu = jnp.concatenate([_pad_to(m, cols=LANE) for m in (mu_wlo, mu_alo, mu_glo)]
                            + [jnp.zeros((1, LANE), F32)], axis=1)
    head_sum = (jnp.arange(WIDTH)[:, None] // RWKV_HEAD == jnp.arange(LANE)[None, :])
    consts = [bf(w_r), bf(w_k), bf(w_v), bf(lo_w), mu_r, mu_k, mu_v, lo_mu,
              _row(p['rwkv_w0'][layer]), bf(_pad_to(p['rwkv_w_w2'][layer], rows=LANE)),
              _row(p['rwkv_a0'][layer]), bf(_pad_to(p['rwkv_a_w2'][layer], rows=LANE)),
              bf(_pad_to(p['rwkv_g_w2'][layer], rows=LANE)),
              _row(p['rwkv_k_k'][layer]), _row(p['rwkv_k_a'][layer]), _row(p['rwkv_r_k'][layer]),
              _row(p['rwkv_ln_g'][layer]), _row(p['rwkv_ln_b'][layer]), bf(head_sum)]
    if has_vres:
        consts += [_row(p['rwkv_v0'][layer - 1]), bf(_pad_to(p['rwkv_v_w2'][layer - 1], rows=LANE))]
    return consts, [_vmem(SUBLANE, WIDTH)] * 4 + [_vmem(RWKV_HEAD, WIDTH)]


def _mixers(x, p, layer, g0, w_cols, v_first):
    (w_sz, w_sxbc, w_sdt, w_gq, w_gk, w_gv, w_gr, w_glo, w_lx, w_lg, w_rw) = w_cols
    parts = [_ssd_operands(p, layer, w_sz, w_sxbc, w_sdt),
             _gla_operands(p, layer, w_gq, w_gk, w_gv, w_gr, w_glo),
             _lru_operands(p, layer, w_lx, w_lg),
             _rwkv_operands(p, layer, w_rw, v_first is not None)]
    consts = [g0] + [c for cs, _ in parts for c in cs]
    scratch = [s for _, ss in parts for s in ss]
    outs = _mixers_call(f"mixers_{layer}", x, consts, v_first, scratch)
    return list(outs[:4]), (v_first if v_first is not None else outs[4])


def _merge(x2d, ys, p, layer, g0, w_gates):
    return _token_call(
        _merge_kernel, f"merge_{layer}", x2d, ys,
        [g0, _row(p['norm_g'][layer, 1]), w_gates.astype(BF16), p['w_branch'][layer].astype(BF16),
         p['w_out'][layer].astype(BF16)])


def _ffn(x2d, p, layer):
    return _token_call(
        _ffn_kernel, f"ffn_{layer}", x2d, [],
        [_row(p['norm_g'][layer, 2]), _row(p['norm_g'][layer, 3]), p['ffn_w_gate'][layer].astype(BF16),
         p['ffn_w_up'][layer].astype(BF16), p['ffn_w_down'][layer].astype(BF16)])


def kernel(x, norm_g, w_in, w_vres_in, ssd_conv_w, ssd_conv_b, ssd_dt_bias, ssd_a_log, ssd_d, ssd_norm_g,
           gla_alpha_w, gla_alpha_b, gla_norm_g, lru_conv_w, lru_conv_b, lru_wa, lru_ba, lru_wx, lru_bx,
           lru_lambda, rwkv_mu, rwkv_w0, rwkv_w_w2, rwkv_a0, rwkv_a_w2, rwkv_v0, rwkv_v_w2, rwkv_g_w2,
           rwkv_k_k, rwkv_k_a, rwkv_r_k, rwkv_ln_g, rwkv_ln_b, w_branch, w_out,
           ffn_w_gate, ffn_w_up, ffn_w_down):
    p = dict(locals())
    bsz, seq, _ = x.shape
    flat = lambda t: t.reshape(bsz * seq, t.shape[-1])
    v_first = None
    for layer in range(w_in.shape[0]):
        g0 = _row(norm_g[layer, 0])
        *w_cols, w_gates = _split_cols(w_in[layer].astype(BF16), IN_SIZES)
        ys, v_first = _mixers(x, p, layer, g0, w_cols, v_first)
        x2d = _merge(flat(x), [flat(y) for y in ys], p, layer, g0, w_gates)
        x = _ffn(x2d, p, layer).reshape(bsz, seq, D_MODEL)
    return x
```

```python
import functools

import jax
import jax.numpy as jnp
from jax import lax
from jax.experimental import pallas as pl
from jax.experimental.pallas import tpu as pltpu

F32 = jnp.float32
BF16 = jnp.bfloat16

D_MODEL = 1024
WIDTH = 512
RMS_EPS = 1e-6
CONV_WIDTH = 4
SSD_HEADS = 8
SSD_HEADDIM = 64
SSD_STATE = 64
SSD_CHUNK = 128
SSD_XBC = WIDTH + 4 * SSD_STATE
GLA_HEADS = 4
GLA_DK = 64
GLA_DV = 128
GLA_LOWRANK = 16
GLA_TAU = 16.0
GLA_CHUNK = 64
LRU_C = 8.0
RWKV_HEADS = 8
RWKV_HEAD = 64
RWKV_CHUNK = 64
RWKV_GN_EPS = 64e-5
RWKV_LORAS = (32, 32, 96)
RWKV_V_LORA = 32
FFN_HIDDEN = 2816
FFN_SPLIT = 2
LANE = 128
SUBLANE = 8
TIME_TILE = 512
TOKEN_TILE = 1024
SUB_TILES = 2
VMEM_LIMIT_BYTES = 56 * 1024 * 1024

IN_SIZES = (WIDTH, SSD_XBC, SSD_HEADS,
            GLA_HEADS * GLA_DK, GLA_HEADS * GLA_DK, WIDTH, WIDTH, GLA_LOWRANK,
            WIDTH, WIDTH,
            3 * WIDTH + sum(RWKV_LORAS),
            4 * D_MODEL)


def _rmsn(x):
    return x * lax.rsqrt(jnp.mean(x * x, axis=-1, keepdims=True) + RMS_EPS)


def _dot(a, b):
    return jnp.dot(a.astype(BF16), b.astype(BF16), preferred_element_type=F32)


def _dot_nt(a, b):
    return lax.dot_general(a.astype(BF16), b.astype(BF16), (((1,), (1,)), ((), ())),
                           preferred_element_type=F32)


def _dot_tn(a, b):
    return lax.dot_general(a.astype(BF16), b.astype(BF16), (((0,), (0,)), ((), ())),
                           preferred_element_type=F32)


def _split(x, parts):
    out = []
    for _ in range(parts - 1):
        hi = x.astype(BF16)
        out.append(hi)
        x = x - hi.astype(F32)
    return out + [x.astype(BF16)]


def _sel_dot(sel, x, parts=3):
    s = sel.astype(BF16)
    return sum(jnp.dot(s, p, preferred_element_type=F32) for p in _split(x, parts))


def _dot_sel(x, sel, parts=3):
    s = sel.astype(BF16)
    return sum(jnp.dot(p, s, preferred_element_type=F32) for p in _split(x, parts))


def _iota(shape, axis):
    return lax.broadcasted_iota(jnp.int32, shape, axis)


def _tril(n, m=None, strict=False):
    m = n if m is None else m
    r, c = _iota((n, m), 0), _iota((n, m), 1)
    return (r > c) if strict else (r >= c)


def _shift_rows(x, tail, j):
    tt, width = x.shape
    r3 = pltpu.roll(x.reshape(tt // SUBLANE, SUBLANE, width), j, 1)
    prev = jnp.concatenate([pltpu.roll(tail, j, 0)[None], r3[:-1]], axis=0)
    return jnp.where(_iota(r3.shape, 1) < j, prev, r3).reshape(tt, width)


def _causal_conv(x, tail, w, b):
    y = x * w[CONV_WIDTH - 1:CONV_WIDTH] + b
    for j in range(1, CONV_WIDTH):
        y = y + _shift_rows(x, tail, j) * w[CONV_WIDTH - 1 - j:CONV_WIDTH - j]
    return y


def _expand_heads(v, n_heads, width):
    rows = v.shape[0]
    return jnp.concatenate(
        [jnp.broadcast_to(v[:, h:h + 1], (rows, width)) for h in range(n_heads)], axis=1)


def _sigmoid(x):
    return jax.nn.sigmoid(x)


def _silu(x):
    return x * jax.nn.sigmoid(x)


def _normed_input(x_ref, g_ref):
    return (_rmsn(x_ref[0]) * g_ref[...]).astype(BF16)


def _ssd_body(h, wz_ref, wxbc_ref, wdt_ref, cw_ref, cb_ref, dtb_ref, alog_ref,
              dsk_ref, ng_ref, o_ref, tail_ref, st_ref, *, tt):
    c_len = SSD_CHUNK
    n_st = 2 * SSD_STATE
    z = jnp.dot(h, wz_ref[...], preferred_element_type=F32)
    yield
    xbc_raw = jnp.dot(h, wxbc_ref[...], preferred_element_type=F32)
    dt_raw = jnp.dot(h, wdt_ref[...], preferred_element_type=F32)
    yield
    xbc = _silu(_causal_conv(xbc_raw, tail_ref[...], cw_ref[...], cb_ref[...]))
    tail_ref[...] = xbc_raw[tt - SUBLANE:]
    dt = jax.nn.softplus(dt_raw + dtb_ref[...])
    da = dt * (-jnp.exp(alog_ref[...]))
    yield

    ltri = _tril(c_len).astype(F32)
    tril = _tril(c_len)
    lane = _iota((c_len, LANE), 1)
    low = lane < SSD_HEADDIM
    st_mask = (_iota((n_st, WIDTH), 0) < SSD_STATE) == (_iota((n_st, WIDTH), 1) < WIDTH // 2)
    chunks = range(tt // c_len)
    rows = lambda t, c: t[c * c_len:(c + 1) * c_len]
    expand = lambda t: _expand_heads(t, SSD_HEADS, SSD_HEADDIM)
    xs = [rows(xbc, c)[:, :WIDTH] for c in chunks]
    bm = [rows(xbc, c)[:, WIDTH:WIDTH + n_st] for c in chunks]
    cm = [rows(xbc, c)[:, WIDTH + n_st:] for c in chunks]
    cum = [_sel_dot(ltri, rows(da, c)) for c in chunks]
    cum_t = [t.T for t in cum]
    dt_t = [rows(dt, c).T for c in chunks]
    yield
    cb = [(_dot_nt(jnp.where(low, cm[c], 0.0), bm[c]), _dot_nt(jnp.where(low, 0.0, cm[c]), bm[c]))
          for c in chunks]
    yield
    upd = [_dot_tn(bm[c], xs[c] * expand(jnp.exp(cum[c][c_len - 1:c_len] - cum[c]) * rows(dt, c)))
           for c in chunks]
    yield
    intra = []
    for c in chunks:
        pairs = []
        for hp in range(SSD_HEADS // 2):
            ws = []
            for hd in (2 * hp, 2 * hp + 1):
                seg = cum[c][:, hd:hd + 1] - cum_t[c][hd:hd + 1, :]
                m = jnp.exp(jnp.where(tril, seg, -jnp.inf)) * dt_t[c][hd:hd + 1, :]
                ws.append((cb[c][hp // 2] * m).astype(BF16))
            xp = xs[c][:, hp * LANE:(hp + 1) * LANE]
            rhs = jnp.concatenate([jnp.where(low, xp, 0.0), jnp.where(low, 0.0, xp)], axis=0)
            pairs.append(_dot(jnp.concatenate(ws, axis=1), rhs))
        intra.append(jnp.concatenate(pairs, axis=1))
        yield
    sts = [st_ref[...]]
    for c in chunks:
        sts.append(sts[c] * expand(jnp.exp(cum[c][c_len - 1:c_len])) + jnp.where(st_mask, upd[c], 0.0))
    st_ref[...] = sts[-1]
    yield
    ys = [intra[c] + _dot(cm[c], sts[c]) * expand(jnp.exp(cum[c])) + dsk_ref[...] * xs[c] for c in chunks]
    yield
    y = jnp.concatenate(ys, axis=0) * _silu(z)
    half = WIDTH // 2
    y = jnp.concatenate([_rmsn(y[:, :half]), _rmsn(y[:, half:])], axis=1)
    o_ref[0] = (y * ng_ref[...]).astype(o_ref.dtype)


def _gla_body(h, wq_ref, wk_ref, wv_ref, wr_ref, wlo_ref, aw_ref, ab_ref, ng_ref,
              o_ref, st_ref, *, tt):
    c_len = GLA_CHUNK
    qk = GLA_HEADS * GLA_DK
    q = jnp.dot(h, wq_ref[...], preferred_element_type=F32) * (GLA_DK ** -0.5)
    k = jnp.dot(h, wk_ref[...], preferred_element_type=F32)
    yield
    v = jnp.dot(h, wv_ref[...], preferred_element_type=F32)
    yield
    r = jnp.dot(h, wr_ref[...], preferred_element_type=F32)
    yield
    lo = jnp.dot(h, wlo_ref[...], preferred_element_type=F32)
    log_a = jax.nn.log_sigmoid(_dot(lo, aw_ref[...]) + ab_ref[...]) / GLA_TAU
    yield

    ltri = _tril(c_len).astype(F32)
    tril_stack = (_iota((GLA_HEADS * c_len, c_len), 0) & (c_len - 1)) >= _iota((GLA_HEADS * c_len, c_len), 1)
    dk_shift = GLA_DK.bit_length() - 1
    head_of_lane = _iota((c_len, qk), 1) >> dk_shift
    head_of_lane_dv = _iota((GLA_DV, qk), 1) >> dk_shift
    chunks = range(tt // c_len)
    heads = range(GLA_HEADS)
    rows = lambda t, c: t[c * c_len:(c + 1) * c_len]
    stack = lambda t: jnp.concatenate([jnp.where(head_of_lane == hd, t, 0.0) for hd in heads], axis=0)
    b = [_sel_dot(ltri, rows(log_a, c)) for c in chunks]
    ref = [t[c_len // 2:c_len // 2 + 1] for t in b]
    b_last = [t[c_len - 1:c_len] for t in b]
    yield
    att = [jnp.where(tril_stack,
                     _dot_nt(stack(rows(q, c) * jnp.exp(b[c] - ref[c])), rows(k, c) * jnp.exp(ref[c] - b[c])),
                     0.0) for c in chunks]
    yield
    full = [_dot_tn(rows(v, c), rows(k, c) * jnp.exp(b_last[c] - b[c])) for c in chunks]
    yield
    sts = [st_ref[...]]
    for c in chunks:
        upd = sum(jnp.where(head_of_lane_dv == hd, full[c][hd * GLA_DV:(hd + 1) * GLA_DV], 0.0)
                  for hd in heads)
        sts.append(sts[c] * jnp.exp(b_last[c]) + upd)
    st_ref[...] = sts[-1]
    yield
    inter = [_dot_nt(stack(rows(q, c) * jnp.exp(b[c])), sts[c]) for c in chunks]
    yield
    ys = [jnp.concatenate(
        [_dot(att[c][hd * c_len:(hd + 1) * c_len], rows(v, c)[:, hd * GLA_DV:(hd + 1) * GLA_DV])
         + inter[c][hd * c_len:(hd + 1) * c_len] for hd in heads], axis=1) for c in chunks]
    yield
    y = jnp.concatenate(ys, axis=0)
    o = jnp.concatenate(
        [_rmsn(y[:, hd * GLA_DV:(hd + 1) * GLA_DV]) * ng_ref[...] for hd in range(GLA_HEADS)], axis=1)
    o_ref[0] = (o * _silu(r)).astype(o_ref.dtype)


def _lru_body(h, wx_ref, wg_ref, cw_ref, cb_ref, wa_ref, ba_ref, wi_ref, bi_ref,
              lam_ref, o_ref, tail_ref, carry_ref, *, tt):
    xr = jnp.dot(h, wx_ref[...], preferred_element_type=F32)
    yield
    gate = jnp.dot(h, wg_ref[...], preferred_element_type=F32)
    yield
    xb = _causal_conv(xr, tail_ref[...], cw_ref[...], cb_ref[...])
    tail_ref[...] = xr[tt - SUBLANE:]
    yield
    r = _sigmoid(_dot(xb, wa_ref[...]) + ba_ref[...])
    yield
    i = _sigmoid(_dot(xb, wi_ref[...]) + bi_ref[...])
    yield
    log_a = -LRU_C * r * jax.nn.softplus(-lam_ref[...])
    a = jnp.exp(log_a)
    u = jnp.sqrt(1.0 - jnp.exp(2.0 * log_a)) * (i * xb)
    yield

    n_blk = tt // SUBLANE
    a = a.reshape(n_blk, SUBLANE, WIDTH)
    u = u.reshape(n_blk, SUBLANE, WIDTH)
    rid = _iota(a.shape, 1)
    s = 1
    while s < SUBLANE:
        m = rid >= s
        u = jnp.where(m, a * pltpu.roll(u, s, 1) + u, u)
        a = jnp.where(m, a * pltpu.roll(a, s, 1), a)
        s *= 2
        yield
    carry = carry_ref[...]
    blocks = []
    for blk in range(n_blk):
        hb = u[blk] + a[blk] * carry
        blocks.append(hb)
        carry = jnp.broadcast_to(hb[SUBLANE - 1:], (SUBLANE, WIDTH))
        if blk % 16 == 15:
            yield
    carry_ref[...] = carry
    hs = jnp.concatenate(blocks, axis=0)
    o_ref[0] = (hs * jax.nn.gelu(gate, approximate=True)).astype(o_ref.dtype)


def _rwkv_body(h, consts, vres, o_ref, vfo_ref, scratch, *, tt):
    (wr_ref, wk_ref, wv_ref, wlo_ref, mur_ref, muk_ref, muv_ref, mulo_ref,
     w0_ref, ww2_ref, a0_ref, aw2_ref, gw2_ref, kk_ref, ka_ref, rk_ref, lng_ref, lnb_ref,
     hsum_ref) = consts
    has_vres = vres is not None
    if has_vres:
        v0_ref, vw2_ref, vf_ref = vres
    (tr_ref, tk_ref, tv_ref, tlo_ref, s_ref) = scratch
    c_len = RWKV_CHUNK
    n = RWKV_HEAD

    def shifted(w_ref, tail_ref, mu_ref):
        p = jnp.dot(h, w_ref[...], preferred_element_type=F32)
        prev = _shift_rows(p, tail_ref[...], 1)
        tail_ref[...] = p[tt - SUBLANE:]
        return p + (prev - p) * mu_ref[...]

    r = shifted(wr_ref, tr_ref, mur_ref)
    yield
    k = shifted(wk_ref, tk_ref, muk_ref)
    yield
    v = shifted(wv_ref, tv_ref, muv_ref)
    yield
    lo = shifted(wlo_ref, tlo_ref, mulo_ref)
    yield
    w_lo, a_lo, g_lo = lo[:, :LANE], lo[:, LANE:2 * LANE], lo[:, 2 * LANE:3 * LANE]
    w_log = -jax.nn.softplus(-(w0_ref[...] + _dot(jnp.tanh(w_lo), ww2_ref[...]))) - 0.5
    log_w = -jnp.exp(w_log)
    a = _sigmoid(a0_ref[...] + _dot(a_lo, aw2_ref[...]))
    g = _dot(_sigmoid(g_lo), gw2_ref[...])
    yield
    if has_vres:
        v = v + (vf_ref[0] - v) * _sigmoid(v0_ref[...] + _dot(lo[:, 3 * LANE:], vw2_ref[...]))
    else:
        vfo_ref[0] = v
    hsum = hsum_ref[...]
    head_sum = lambda t: _dot_sel(t, hsum, parts=2)
    per_head = lambda t: _expand_heads(t, RWKV_HEADS, n)
    kk = k * kk_ref[...]
    kk = kk * per_head(1.0 / jnp.maximum(jnp.sqrt(head_sum(kk * kk)), 1e-12))
    k = k * (1.0 + (a - 1.0) * ka_ref[...])
    bonus = per_head(head_sum(r * k * rk_ref[...])) * v
    yield

    n_chunks = tt // c_len
    n_pairs = WIDTH // LANE
    ltri = _tril(c_len).astype(F32)
    lc = jnp.concatenate(
        [_sel_dot(ltri, log_w[c * c_len:(c + 1) * c_len]) for c in range(n_chunks)], axis=0)
    yield
    e_pos = jnp.exp(lc)
    e_neg = jnp.exp(-lc)
    rt = r * e_pos
    at = -kk * jnp.exp(lc - log_w)
    bt = kk * a * e_neg
    kt = k * e_neg
    yield

    row = _iota((c_len, LANE), 0)
    col = _iota((c_len, LANE), 1) & (n - 1)
    stril = row > col
    eye = (row == col).astype(F32)
    tril2 = _iota((c_len, 2 * LANE), 0) >= (_iota((c_len, 2 * LANE), 1) & (n - 1))
    low = _iota((c_len, LANE), 1) < n

    def bd(t):
        return jnp.concatenate([jnp.where(low, t, 0.0), jnp.where(low, 0.0, t)], axis=0)

    blocks = [(c, q) for c in range(n_chunks) for q in range(n_pairs)]
    blk = lambda t, cq: t[cq[0] * c_len:(cq[0] + 1) * c_len, cq[1] * LANE:(cq[1] + 1) * LANE]

    e_last_t = jnp.concatenate(
        [e_pos[(c + 1) * c_len - 1:(c + 1) * c_len] for c in range(n_chunks)], axis=0).T

    def decay_cols(cq):
        c, q = cq
        col_of = lambda hd: jnp.broadcast_to(e_last_t[hd * n:(hd + 1) * n, c:c + 1], (c_len, LANE))
        return jnp.where(low, col_of(2 * q), col_of(2 * q + 1))

    vbd = [bd(blk(v, cq)) for cq in blocks]
    yield
    m = [_dot_nt(jnp.concatenate([blk(at, cq), blk(rt, cq), eye], axis=0),
                 jnp.concatenate([bd(blk(bt, cq)), bd(blk(kt, cq))], axis=0)) for cq in blocks]
    yield
    p = [jnp.where(stril, mb[:c_len, :LANE], 0.0) for mb in m]
    rbk = [jnp.concatenate([jnp.where(tril2, mb[c_len:2 * c_len], 0.0), mb[2 * c_len:]], axis=0) for mb in m]
    lakv = [_dot(jnp.where(stril, mb[:c_len, LANE:], 0.0), vb) for mb, vb in zip(m, vbd)]
    yield
    inv = [eye + pb for pb in p]
    p = [_dot(pb, bd(pb)) for pb in p]
    yield
    span = 4
    while span < c_len:
        px = [_dot(jnp.concatenate([pb, ib], axis=0), bd(pb)) for pb, ib in zip(p, inv)]
        yield
        p = [t[:c_len] for t in px]
        inv = [ib + t[c_len:] for ib, t in zip(inv, px)]
        span *= 2
    inv = [ib + _dot(ib, bd(pb)) for ib, pb in zip(inv, p)]
    yield
    tz = [_dot(ib, jnp.concatenate([bd(blk(at, cq)), bd(lb)], axis=1))
          for ib, lb, cq in zip(inv, lakv, blocks)]
    ar = [jnp.concatenate([tzb[:, :LANE], blk(rt, cq)], axis=0) for tzb, cq in zip(tz, blocks)]
    yield
    e_cols = [decay_cols(cq) for cq in blocks]
    yield

    st = [s_ref[:, q * LANE:(q + 1) * LANE] for q in range(n_pairs)]
    y_rows = []
    for c in range(n_chunks):
        ids = range(c * n_pairs, (c + 1) * n_pairs)
        gs = [_dot(ar[i], bd(st[q])) for q, i in enumerate(ids)]
        yield
        uv = [jnp.concatenate([bd(gs[q][:c_len] + tz[i][:, LANE:]), vbd[i]], axis=0) for q, i in enumerate(ids)]
        yd = [_dot(rbk[i], uv[q]) for q, i in enumerate(ids)]
        y_rows.append(jnp.concatenate([gs[q][c_len:] + yd[q][:c_len] for q in range(n_pairs)], axis=1))
        st = [(st[q] + yd[q][c_len:]) * e_cols[i] for q, i in enumerate(ids)]
        yield
    s_ref[...] = jnp.concatenate(st, axis=1)

    y = jnp.concatenate(y_rows, axis=0)
    mean = per_head(head_sum(y) * (1.0 / n))
    yield
    d = y - mean
    inv_std = per_head(lax.rsqrt(head_sum(d * d) * (1.0 / n) + RWKV_GN_EPS))
    yield
    y = d * inv_std * lng_ref[...] + lnb_ref[...] + bonus
    o_ref[0] = (y * g).astype(o_ref.dtype)


N_SSD_CONSTS, N_GLA_CONSTS, N_LRU_CONSTS, N_RWKV_CONSTS = 9, 8, 9, 19
N_SSD_SCRATCH, N_GLA_SCRATCH, N_LRU_SCRATCH, N_RWKV_SCRATCH = 2, 1, 2, 5
MIX_STEPS = (4, 1, 1, 1)
MIX_DELAY = (0, 0, 0, 6)


def _interleave(gens):
    live = list(gens)
    rnd = 0
    while live:
        nxt = []
        for gen, steps, delay in live:
            if rnd < delay:
                nxt.append((gen, steps, delay))
                continue
            for _ in range(steps):
                if next(gen, StopIteration) is StopIteration:
                    break
            else:
                nxt.append((gen, steps, delay))
        live = nxt
        rnd += 1


def _mixers_kernel(*refs, tt, has_vres):
    refs = list(refs)
    take = lambda k: [refs.pop(0) for _ in range(k)]
    x_ref, g_ref = take(2)
    ssd_c, gla_c, lru_c, rwkv_c = (take(k) for k in (N_SSD_CONSTS, N_GLA_CONSTS, N_LRU_CONSTS, N_RWKV_CONSTS))
    vres = take(3) if has_vres else None
    y_ssd, y_gla, y_lru, y_rwkv = take(4)
    vfo_ref = None if has_vres else take(1)[0]
    ssd_s, gla_s, lru_s, rwkv_s = (take(k) for k in (N_SSD_SCRATCH, N_GLA_SCRATCH, N_LRU_SCRATCH, N_RWKV_SCRATCH))

    @pl.when(pl.program_id(1) == 0)
    def _():
        for ref in ssd_s + gla_s + lru_s + rwkv_s:
            ref[...] = jnp.zeros_like(ref)

    h = _normed_input(x_ref, g_ref)
    _interleave([(_rwkv_body(h, rwkv_c, vres, y_rwkv, vfo_ref, rwkv_s, tt=tt), MIX_STEPS[0], MIX_DELAY[0]),
                 (_ssd_body(h, *ssd_c, y_ssd, *ssd_s, tt=tt), MIX_STEPS[1], MIX_DELAY[1]),
                 (_gla_body(h, *gla_c, y_gla, *gla_s, tt=tt), MIX_STEPS[2], MIX_DELAY[2]),
                 (_lru_body(h, *lru_c, y_lru, *lru_s, tt=tt), MIX_STEPS[3], MIX_DELAY[3])])


def _sub_tiles(tm):
    step = tm // SUB_TILES
    return [slice(s * step, (s + 1) * step) for s in range(SUB_TILES)]


def _merge_rows(rows, x_ref, y_refs, g0_ref, g1_ref, wgate_ref, wbr_ref, wout_ref, o_ref):
    x = x_ref[rows, :]
    h = (_rmsn(x) * g0_ref[...]).astype(BF16)
    yield
    acc = jnp.zeros(x.shape, F32)
    for kb, y_ref in enumerate(y_refs):
        logits = jnp.dot(h, wgate_ref[:, kb * D_MODEL:(kb + 1) * D_MODEL], preferred_element_type=F32)
        branch = jnp.dot(y_ref[rows, :], wbr_ref[kb], preferred_element_type=F32)
        yield
        acc = acc + _sigmoid(logits) * branch
    o = jnp.dot(acc.astype(BF16), wout_ref[...], preferred_element_type=F32)
    yield
    o_ref[rows, :] = x + _rmsn(o) * g1_ref[...]


def _merge_kernel(x_ref, y0_ref, y1_ref, y2_ref, y3_ref, g0_ref, g1_ref, wgate_ref, wbr_ref, wout_ref,
                  o_ref):
    _interleave([(_merge_rows(rows, x_ref, (y0_ref, y1_ref, y2_ref, y3_ref), g0_ref, g1_ref,
                              wgate_ref, wbr_ref, wout_ref, o_ref), 1, s)
                 for s, rows in enumerate(_sub_tiles(x_ref.shape[0]))])


def _ffn_rows(rows, x_ref, g2_ref, g3_ref, wg_ref, wu_ref, wd_ref, o_ref):
    x = x_ref[rows, :]
    h = (_rmsn(x) * g2_ref[...]).astype(BF16)
    yield
    f = jnp.zeros(x.shape, F32)
    step = FFN_HIDDEN // FFN_SPLIT
    for c in range(FFN_SPLIT):
        cols = slice(c * step, (c + 1) * step)
        gt = jnp.dot(h, wg_ref[:, cols], preferred_element_type=F32)
        up = jnp.dot(h, wu_ref[:, cols], preferred_element_type=F32)
        yield
        f = f + jnp.dot((_silu(gt) * up).astype(BF16), wd_ref[cols, :], preferred_element_type=F32)
        yield
    o_ref[rows, :] = x + _rmsn(f) * g3_ref[...]


def _ffn_kernel(x_ref, g2_ref, g3_ref, wg_ref, wu_ref, wd_ref, o_ref):
    _interleave([(_ffn_rows(rows, x_ref, g2_ref, g3_ref, wg_ref, wu_ref, wd_ref, o_ref), 1, s)
                 for s, rows in enumerate(_sub_tiles(x_ref.shape[0]))])


def _const_spec(a):
    nd = a.ndim
    return pl.BlockSpec(a.shape, lambda *_: (0,) * nd, pipeline_mode=pl.Buffered(1))


def _time_tile(seq):
    return min(TIME_TILE, seq)


def _mixers_call(name, x, consts, v_first, scratch):
    bsz, seq, _ = x.shape
    tt = _time_tile(seq)
    assert seq % tt == 0 and tt % SSD_CHUNK == 0
    has_vres = v_first is not None
    tile = lambda width: pl.BlockSpec((1, tt, width), lambda b, t: (b, t, 0))
    tiles_in = [x] + ([v_first] if has_vres else [])
    in_specs = [tile(D_MODEL)] + [_const_spec(c) for c in consts] + ([tile(WIDTH)] if has_vres else [])
    out_shape = [jax.ShapeDtypeStruct((bsz, seq, WIDTH), BF16)] * 4
    if not has_vres:
        out_shape.append(jax.ShapeDtypeStruct((bsz, seq, WIDTH), F32))
    return pl.pallas_call(
        functools.partial(_mixers_kernel, tt=tt, has_vres=has_vres),
        grid=(bsz, seq // tt),
        in_specs=in_specs,
        out_specs=[tile(WIDTH)] * len(out_shape),
        out_shape=out_shape,
        scratch_shapes=list(scratch),
        compiler_params=pltpu.CompilerParams(
            dimension_semantics=("parallel", "arbitrary"), vmem_limit_bytes=VMEM_LIMIT_BYTES),
        name=name,
    )(tiles_in[0], *consts, *tiles_in[1:])


def _token_call(body, name, x2d, tiles, consts):
    tokens = x2d.shape[0]
    tm = min(TOKEN_TILE, tokens)
    assert tokens % tm == 0
    tile = lambda width: pl.BlockSpec((tm, width), lambda i: (i, 0))
    return pl.pallas_call(
        body,
        grid=(tokens // tm,),
        in_specs=[tile(D_MODEL)] + [tile(t.shape[-1]) for t in tiles] + [_const_spec(c) for c in consts],
        out_specs=tile(D_MODEL),
        out_shape=jax.ShapeDtypeStruct(x2d.shape, F32),
        compiler_params=pltpu.CompilerParams(
            dimension_semantics=("parallel",), vmem_limit_bytes=VMEM_LIMIT_BYTES),
        name=name,
    )(x2d, *tiles, *consts)


def _row(v, width=None):
    v = v.reshape(1, -1).astype(F32)
    if width is not None and v.shape[1] < width:
        v = jnp.pad(v, ((0, 0), (0, width - v.shape[1])))
    return v


def _pad_to(w, rows=None, cols=None):
    rows = w.shape[0] if rows is None else rows
    cols = w.shape[1] if cols is None else cols
    return jnp.pad(w, ((0, rows - w.shape[0]), (0, cols - w.shape[1])))


def _split_cols(w, sizes):
    out, start = [], 0
    for nsz in sizes:
        out.append(w[:, start:start + nsz])
        start += nsz
    return out


def _block_diag(w):
    nb, bi, bj = w.shape
    eye = jnp.eye(nb, dtype=w.dtype)
    return jnp.einsum('hij,hg->higj', w, eye).reshape(nb * bi, nb * bj)


def _vmem(*shape):
    return pltpu.VMEM(shape, F32)


def _ssd_operands(p, layer, w_z, w_xbc, w_dt):
    consts = [w_z.astype(BF16), w_xbc.astype(BF16), _pad_to(w_dt, cols=LANE).astype(BF16),
              p['ssd_conv_w'][layer], _row(p['ssd_conv_b'][layer]), _row(p['ssd_dt_bias'][layer], LANE),
              _row(p['ssd_a_log'][layer], LANE), _row(jnp.repeat(p['ssd_d'][layer], SSD_HEADDIM)),
              _row(p['ssd_norm_g'][layer])]
    return consts, [_vmem(SUBLANE, SSD_XBC), _vmem(2 * SSD_STATE, WIDTH)]


def _gla_operands(p, layer, w_q, w_k, w_v, w_r, w_lo):
    consts = [w_q.astype(BF16), w_k.astype(BF16), w_v.astype(BF16), w_r.astype(BF16),
              _pad_to(w_lo, cols=LANE).astype(BF16), _pad_to(p['gla_alpha_w'][layer], rows=LANE).astype(BF16),
              _row(p['gla_alpha_b'][layer]), _row(p['gla_norm_g'][layer])]
    return consts, [_vmem(GLA_DV, GLA_HEADS * GLA_DK)]


def _lru_operands(p, layer, w_x, w_gate):
    consts = [w_x.astype(BF16), w_gate.astype(BF16), p['lru_conv_w'][layer], _row(p['lru_conv_b'][layer]),
              _block_diag(p['lru_wa'][layer]).astype(BF16), _row(p['lru_ba'][layer]),
              _block_diag(p['lru_wx'][layer]).astype(BF16), _row(p['lru_bx'][layer]),
              _row(p['lru_lambda'][layer])]
    return consts, [_vmem(SUBLANE, WIDTH)] * 2


def _rwkv_operands(p, layer, w_rw, has_vres):
    bf = lambda w: w.astype(BF16)
    sizes = (WIDTH, WIDTH, WIDTH) + RWKV_LORAS
    w_r, w_k, w_v, w_wlo, w_alo, w_glo = _split_cols(w_rw, sizes)
    mu_r, mu_k, mu_v, mu_wlo, mu_alo, mu_glo = _split_cols(p['rwkv_mu'][layer].reshape(1, -1), sizes)
    w_vlo = p['w_vres_in'][layer - 1] if has_vres else jnp.zeros((D_MODEL, RWKV_V_LORA), F32)
    lo_w = jnp.concatenate([_pad_to(w, cols=LANE) for w in (w_wlo, w_alo, w_glo, w_vlo)], axis=1)
    lo_mu = jnp.concatenate([_pad_to(m, cols=LANE) for m in (mu_wlo, mu_alo, mu_glo)]
                            + [jnp.zeros((1, LANE), F32)], axis=1)
    head_sum = (jnp.arange(WIDTH)[:, None] // RWKV_HEAD == jnp.arange(LANE)[None, :])
    consts = [bf(w_r), bf(w_k), bf(w_v), bf(lo_w), mu_r, mu_k, mu_v, lo_mu,
              _row(p['rwkv_w0'][layer]), bf(_pad_to(p['rwkv_w_w2'][layer], rows=LANE)),
              _row(p['rwkv_a0'][layer]), bf(_pad_to(p['rwkv_a_w2'][layer], rows=LANE)),
              bf(_pad_to(p['rwkv_g_w2'][layer], rows=LANE)),
              _row(p['rwkv_k_k'][layer]), _row(p['rwkv_k_a'][layer]), _row(p['rwkv_r_k'][layer]),
              _row(p['rwkv_ln_g'][layer]), _row(p['rwkv_ln_b'][layer]), bf(head_sum)]
    if has_vres:
        consts += [_row(p['rwkv_v0'][layer - 1]), bf(_pad_to(p['rwkv_v_w2'][layer - 1], rows=LANE))]
    return consts, [_vmem(SUBLANE, WIDTH)] * 4 + [_vmem(RWKV_HEAD, WIDTH)]


def _mixers(x, p, layer, g0, w_cols, v_first):
    (w_sz, w_sxbc, w_sdt, w_gq, w_gk, w_gv, w_gr, w_glo, w_lx, w_lg, w_rw) = w_cols
    parts = [_ssd_operands(p, layer, w_sz, w_sxbc, w_sdt),
             _gla_operands(p, layer, w_gq, w_gk, w_gv, w_gr, w_glo),
             _lru_operands(p, layer, w_lx, w_lg),
             _rwkv_operands(p, layer, w_rw, v_first is not None)]
    consts = [g0] + [c for cs, _ in parts for c in cs]
    scratch = [s for _, ss in parts for s in ss]
    outs = _mixers_call(f"mixers_{layer}", x, consts, v_first, scratch)
    return list(outs[:4]), (v_first if v_first is not None else outs[4])


def _merge(x2d, ys, p, layer, g0, w_gates):
    return _token_call(
        _merge_kernel, f"merge_{layer}", x2d, ys,
        [g0, _row(p['norm_g'][layer, 1]), w_gates.astype(BF16), p['w_branch'][layer].astype(BF16),
         p['w_out'][layer].astype(BF16)])


def _ffn(x2d, p, layer):
    return _token_call(
        _ffn_kernel, f"ffn_{layer}", x2d, [],
        [_row(p['norm_g'][layer, 2]), _row(p['norm_g'][layer, 3]), p['ffn_w_gate'][layer].astype(BF16),
         p['ffn_w_up'][layer].astype(BF16), p['ffn_w_down'][layer].astype(BF16)])


def kernel(x, norm_g, w_in, w_vres_in, ssd_conv_w, ssd_conv_b, ssd_dt_bias, ssd_a_log, ssd_d, ssd_norm_g,
           gla_alpha_w, gla_alpha_b, gla_norm_g, lru_conv_w, lru_conv_b, lru_wa, lru_ba, lru_wx, lru_bx,
           lru_lambda, rwkv_mu, rwkv_w0, rwkv_w_w2, rwkv_a0, rwkv_a_w2, rwkv_v0, rwkv_v_w2, rwkv_g_w2,
           rwkv_k_k, rwkv_k_a, rwkv_r_k, rwkv_ln_g, rwkv_ln_b, w_branch, w_out,
           ffn_w_gate, ffn_w_up, ffn_w_down):
    p = dict(locals())
    bsz, seq, _ = x.shape
    flat = lambda t: t.reshape(bsz * seq, t.shape[-1])
    v_first = None
    for layer in range(w_in.shape[0]):
        g0 = _row(norm_g[layer, 0])
        *w_cols, w_gates = _split_cols(w_in[layer], IN_SIZES)
        ys, v_first = _mixers(x, p, layer, g0, w_cols, v_first)
        x2d = _merge(flat(x), [flat(y) for y in ys], p, layer, g0, w_gates)
        x = _ffn(x2d, p, layer).reshape(bsz, seq, D_MODEL)
    return x
```
